```python
import jax, jax.numpy as jnp
from jax import lax
import numpy as np

D_MODEL = 4096
BATCH = 4
SEQ = 4096
DEPTH = 4

BLOCK = 128
RMS_EPS = 1e-6
HEAD_DIM_AB = 128
A_PAIRS = ((128, 1), (512, 4), (2048, 16))
A_HEADS_PER_GROUP = D_MODEL // 512
A_HEADS = A_HEADS_PER_GROUP * len(A_PAIRS)
B_HEADS = D_MODEL // 512
A_HW = A_HEADS * HEAD_DIM_AB
B_HW = B_HEADS * HEAD_DIM_AB
AB_IN = 3 * A_HW + 3 * B_HW
AB_OUT = (A_HEADS_PER_GROUP + B_HEADS) * HEAD_DIM_AB
C_HEAD_DIM = 64
C_Q_HEADS = D_MODEL // C_HEAD_DIM
C_KV_HEADS = 8
C_WINDOW = 128
C_IN = (C_Q_HEADS + 2 * C_KV_HEADS) * C_HEAD_DIM
C_OUT = C_Q_HEADS * C_HEAD_DIM
D_FF = 3 * D_MODEL // 2
N_EVEN = (DEPTH + 1) // 2
N_ODD = DEPTH // 2

kernel_name = "hybrid_dilated_stickbreak_swa_macaron"


def alibi_slopes(n):
    return jnp.asarray(2.0 ** (-8.0 * np.arange(1, n + 1) / n), dtype=jnp.float32)


def rms_norm(x, g):
    xf = x.astype(jnp.float32)
    y = xf * lax.rsqrt(jnp.mean(xf * xf, axis=-1, keepdims=True) + RMS_EPS)
    return (y * g.astype(jnp.float32)).astype(x.dtype)


def swiglu(x, w_in, w_out):
    gate, up = jnp.split(x @ w_in, 2, axis=-1)
    return (jax.nn.silu(gate) * up) @ w_out


def banded_attention(q, k, v, max_dist, slopes, dist_scale, sinks=None):
    N, L, H, hd = q.shape
    Hkv = k.shape[2]
    G = H // Hkv
    nb = -(-L // BLOCK)
    Lp = nb * BLOCK
    padw = ((0, 0), (0, Lp - L), (0, 0), (0, 0))
    qb = jnp.pad(q, padw).reshape(N, nb, BLOCK, Hkv, G, hd)
    kb = jnp.pad(k, padw).reshape(N, nb, BLOCK, Hkv, hd)
    vb = jnp.pad(v, padw).reshape(N, nb, BLOCK, Hkv, hd)
    kcat = jnp.concatenate([jnp.concatenate([jnp.zeros_like(kb[:, :1]), kb[:, :-1]], axis=1), kb], axis=2)
    vcat = jnp.concatenate([jnp.concatenate([jnp.zeros_like(vb[:, :1]), vb[:, :-1]], axis=1), vb], axis=2)
    s = jnp.einsum('nbqkgd,nbskd->nbkgqs', qb, kcat).astype(jnp.float32) * (hd ** -0.5)
    qi = jnp.arange(BLOCK)[:, None]
    si = jnp.arange(2 * BLOCK)[None, :]
    dist = qi - si + BLOCK
    key_pos = jnp.arange(nb)[:, None, None] * BLOCK - BLOCK + si[None]
    valid = (dist >= 0) & (dist <= max_dist) & (key_pos >= 0)
    bias = -(slopes.reshape(Hkv, G)[:, :, None, None] * (dist_scale * dist.astype(jnp.float32)))
    s = jnp.where(valid[None, :, None, None], s + bias, -jnp.inf)
    m = jnp.max(s, axis=-1)
    if sinks is not None:
        sk = sinks.astype(jnp.float32).reshape(Hkv, G)[:, :, None]
        m = jnp.maximum(m, sk)
    p = jnp.exp(s - m[..., None])
    denom = jnp.sum(p, axis=-1)
    if sinks is not None:
        denom = denom + jnp.exp(sk - m)
    o = jnp.einsum('nbkgqs,nbskd->nbqkgd', p, vcat.astype(jnp.float32))
    o = o / jnp.transpose(denom, (0, 1, 4, 2, 3))[..., None]
    lse = jnp.transpose(m + jnp.log(denom), (0, 1, 4, 2, 3)).reshape(N, Lp, H)
    return o.reshape(N, Lp, H, hd)[:, :L].astype(q.dtype), lse[:, :L]


def dilate(t, r):
    B, S = t.shape[:2]
    rest = t.shape[2:]
    return t.reshape((B, S // r, r) + rest).swapaxes(1, 2).reshape((B * r, S // r) + rest)


def undilate(t, r, B):
    L = t.shape[1]
    rest = t.shape[2:]
    return t.reshape((B, r, L) + rest).swapaxes(1, 2).reshape((B, r * L) + rest)


def stick_breaking(q, k, v):
    B, S, H, hd = q.shape
    outs = []
    for b in range(S // BLOCK):
        e = (b + 1) * BLOCK
        z = jnp.einsum('bqhd,bshd->bhqs', q[:, b * BLOCK:e], k[:, :e]).astype(jnp.float32) * (hd ** -0.5)
        causal = jnp.arange(e)[None, :] < (b * BLOCK + jnp.arange(BLOCK))[:, None]
        log_keep = jnp.where(causal, jax.nn.log_sigmoid(-z), 0.0)
        after = lax.cumsum(log_keep, axis=3, reverse=True) - log_keep
        a = jnp.where(causal, jnp.exp(jax.nn.log_sigmoid(z) + after), 0.0)
        outs.append(jnp.einsum('bhqs,bshd->bqhd', a, v[:, :e].astype(jnp.float32)))
    return jnp.concatenate(outs, axis=1).astype(q.dtype)


def mixer_ab(h, w_in, w_out):
    B, S, _ = h.shape
    cuts = [A_HW, 2 * A_HW, 3 * A_HW, 3 * A_HW + B_HW, 3 * A_HW + 2 * B_HW]
    qa, ka, va, qb, kb, vb = jnp.split(h @ w_in, cuts, axis=-1)
    qa, ka, va = (t.reshape(B, S, A_HEADS, HEAD_DIM_AB) for t in (qa, ka, va))
    qb, kb, vb = (t.reshape(B, S, B_HEADS, HEAD_DIM_AB) for t in (qb, kb, vb))
    slopes = alibi_slopes(A_HEADS)
    outs, lses = [], []
    for gi, (window, r) in enumerate(A_PAIRS):
        sl = slice(gi * A_HEADS_PER_GROUP, (gi + 1) * A_HEADS_PER_GROUP)
        o, l = banded_attention(dilate(qa[:, :, sl], r), dilate(ka[:, :, sl], r), dilate(va[:, :, sl], r),
                                window // r, slopes[sl], r)
        outs.append(undilate(o, r, B))
        lses.append(undilate(l, r, B))
    wts = jax.nn.softmax(jnp.stack(lses), axis=0)
    out_a = jnp.sum(wts[..., None] * jnp.stack(outs).astype(jnp.float32), axis=0).astype(h.dtype)
    out_b = stick_breaking(qb, kb, vb)
    cat = jnp.concatenate([out_a.reshape(B, S, -1), out_b.reshape(B, S, -1)], axis=-1)
    return cat @ w_out


def mixer_c(h, w_in, w_out, sinks):
    B, S, _ = h.shape
    kvw = C_KV_HEADS * C_HEAD_DIM
    q, k, v = jnp.split(h @ w_in, [C_OUT, C_OUT + kvw], axis=-1)
    q = q.reshape(B, S, C_Q_HEADS, C_HEAD_DIM)
    k = k.reshape(B, S, C_KV_HEADS, C_HEAD_DIM)
    v = v.reshape(B, S, C_KV_HEADS, C_HEAD_DIM)
    o, _ = banded_attention(q, k, v, C_WINDOW - 1, alibi_slopes(C_Q_HEADS), 1.0, sinks)
    return o.reshape(B, S, C_OUT) @ w_out


def setup_inputs(seed: int = 0) -> dict:
    key = jax.random.key(seed)
    ks = jax.random.split(key, 16)
    f32 = jnp.float32

    def w(k, shape, fan_in):
        return jax.random.normal(k, shape, f32) * (fan_in ** -0.5)

    def gain(k, shape):
        return 1.0 + 0.01 * jax.random.normal(k, shape, f32)

    return {
        "x": jax.random.normal(ks[0], (BATCH, SEQ, D_MODEL), f32),
        "ffn1_norm": gain(ks[1], (DEPTH, D_MODEL)),
        "ffn1_w_in": w(ks[2], (DEPTH, D_MODEL, 2 * D_FF), D_MODEL),
        "ffn1_w_out": w(ks[3], (DEPTH, D_FF, D_MODEL), D_FF),
        "mix_norm": gain(ks[4], (DEPTH, D_MODEL)),
        "ab_w_in": w(ks[5], (N_EVEN, D_MODEL, AB_IN), D_MODEL),
        "ab_w_out": w(ks[6], (N_EVEN, AB_OUT, D_MODEL), AB_OUT),
        "c_w_in": w(ks[7], (N_ODD, D_MODEL, C_IN), D_MODEL),
        "c_w_out": w(ks[8], (N_ODD, C_OUT, D_MODEL), C_OUT),
        "c_sinks": 0.5 * jax.random.normal(ks[9], (N_ODD, C_Q_HEADS), f32),
        "ffn2_norm": gain(ks[10], (DEPTH, D_MODEL)),
        "ffn2_w_in": w(ks[11], (DEPTH, D_MODEL, 2 * D_FF), D_MODEL),
        "ffn2_w_out": w(ks[12], (DEPTH, D_FF, D_MODEL), D_FF),
        "final_norm": gain(ks[13], (D_MODEL,)),
    }


def reference(x, ffn1_norm, ffn1_w_in, ffn1_w_out, mix_norm, ab_w_in, ab_w_out, c_w_in, c_w_out,
              c_sinks, ffn2_norm, ffn2_w_in, ffn2_w_out, final_norm):
    for l in range(DEPTH):
        x = x + 0.5 * swiglu(rms_norm(x, ffn1_norm[l]), ffn1_w_in[l], ffn1_w_out[l])
        h = rms_norm(x, mix_norm[l])
        if l % 2 == 0:
            x = x + mixer_ab(h, ab_w_in[l // 2], ab_w_out[l // 2])
        else:
            x = x + mixer_c(h, c_w_in[l // 2], c_w_out[l // 2], c_sinks[l // 2])
        x = x + 0.5 * swiglu(rms_norm(x, ffn2_norm[l]), ffn2_w_in[l], ffn2_w_out[l])
    return rms_norm(x, final_norm)
```

```python
import functools

import numpy as np
import jax
import jax.numpy as jnp
from jax import lax
from jax.experimental import pallas as pl
from jax.experimental.pallas import tpu as pltpu

F32 = jnp.float32
BF16 = jnp.bfloat16

BLOCK = 128
RMS_EPS = 1e-6
HEAD_DIM_AB = 128
A_PAIRS = ((128, 1), (512, 4), (2048, 16))
GROUP_HEADS = 8
A_HEADS = GROUP_HEADS * len(A_PAIRS)
GROUP_W = GROUP_HEADS * HEAD_DIM_AB
C_HEAD_DIM = 64
C_Q_HEADS = 64
C_KV_HEADS = 8
C_WINDOW = 128

LANES = 128
VMEM_LIMIT_BYTES = 56 * 1024 * 1024


def _alibi_slopes(n):
    return [float(v) for v in np.asarray(2.0 ** (-8.0 * np.arange(1, n + 1) / n), dtype=np.float32)]


def _params(*sem):
    return pltpu.CompilerParams(dimension_semantics=sem, vmem_limit_bytes=VMEM_LIMIT_BYTES)


def _rmsnorm_kernel(x_ref, g_ref, o_ref):
    x = x_ref[...]
    ms = jnp.mean(x * x, axis=-1, keepdims=True)
    o_ref[...] = (x * lax.rsqrt(ms + RMS_EPS) * g_ref[...]).astype(o_ref.dtype)


def _rmsnorm(x, g, out_dtype, tm=256):
    t, d = x.shape
    return pl.pallas_call(
        _rmsnorm_kernel,
        grid=(t // tm,),
        in_specs=[pl.BlockSpec((tm, d), lambda i: (i, 0)), pl.BlockSpec((1, d), lambda i: (0, 0))],
        out_specs=pl.BlockSpec((tm, d), lambda i: (i, 0)),
        out_shape=jax.ShapeDtypeStruct((t, d), out_dtype),
        compiler_params=_params("parallel"),
        name="rmsnorm",
    )(x, g.reshape(1, d).astype(F32))


def _mm_kernel(a_ref, w_ref, o_ref):
    o_ref[...] = jnp.dot(a_ref[...], w_ref[...], preferred_element_type=F32).astype(o_ref.dtype)


def _mm_swiglu_kernel(a_ref, wg_ref, wu_ref, o_ref):
    a = a_ref[...]
    g = jnp.dot(a, wg_ref[...], preferred_element_type=F32)
    u = jnp.dot(a, wu_ref[...], preferred_element_type=F32)
    o_ref[...] = (g * jax.nn.sigmoid(g) * u).astype(o_ref.dtype)


def _mm_residual_kernel(a_ref, w_ref, x_ref, o_ref, *, scale):
    acc = jnp.dot(a_ref[...], w_ref[...], preferred_element_type=F32)
    o_ref[...] = x_ref[...] + scale * acc


def _matmul(a, w, tm, tn):
    t, k = a.shape
    n = w.shape[1]
    return pl.pallas_call(
        _mm_kernel,
        grid=(t // tm, n // tn),
        in_specs=[pl.BlockSpec((tm, k), lambda i, j: (i, 0)), pl.BlockSpec((k, tn), lambda i, j: (0, j))],
        out_specs=pl.BlockSpec((tm, tn), lambda i, j: (i, j)),
        out_shape=jax.ShapeDtypeStruct((t, n), BF16),
        compiler_params=_params("parallel", "arbitrary"),
        name="matmul",
    )(a, w)


def _matmul_swiglu(a, w_in, tm, tn):
    t, k = a.shape
    f = w_in.shape[1] // 2
    nj = f // tn
    return pl.pallas_call(
        _mm_swiglu_kernel,
        grid=(t // tm, nj),
        in_specs=[
            pl.BlockSpec((tm, k), lambda i, j: (i, 0)),
            pl.BlockSpec((k, tn), lambda i, j: (0, j)),
            pl.BlockSpec((k, tn), lambda i, j: (0, j + nj)),
        ],
        out_specs=pl.BlockSpec((tm, tn), lambda i, j: (i, j)),
        out_shape=jax.ShapeDtypeStruct((t, f), BF16),
        compiler_params=_params("parallel", "arbitrary"),
        name="matmul_swiglu",
    )(a, w_in, w_in)


def _matmul_residual(a, w, x, scale, tm, tn):
    t, k = a.shape
    n = w.shape[1]
    return pl.pallas_call(
        functools.partial(_mm_residual_kernel, scale=scale),
        grid=(t // tm, n // tn),
        in_specs=[
            pl.BlockSpec((tm, k), lambda i, j: (i, 0)),
            pl.BlockSpec((k, tn), lambda i, j: (0, j)),
            pl.BlockSpec((tm, tn), lambda i, j: (i, j)),
        ],
        out_specs=pl.BlockSpec((tm, tn), lambda i, j: (i, j)),
        out_shape=jax.ShapeDtypeStruct((t, n), F32),
        input_output_aliases={2: 0},
        compiler_params=_params("parallel", "arbitrary"),
        name="matmul_residual",
    )(a, w, x)


def _band_masks(first_block_offset, max_dist, width):
    qi = lax.broadcasted_iota(jnp.int32, (BLOCK, width), 0)
    si = lax.broadcasted_iota(jnp.int32, (BLOCK, width), 1) & (2 * BLOCK - 1)
    dist = qi - si + BLOCK
    valid = (dist >= 0) & (dist <= max_dist) & (si + first_block_offset >= 0)
    return dist.astype(F32), valid


def _attn_a_kernel(q_ref, kp_ref, kc_ref, vp_ref, vc_ref, o_ref, l_ref, *, slopes, dist_scale, max_dist):
    i = pl.program_id(2)
    dist, valid = _band_masks((i - 1) * BLOCK, max_dist, 2 * BLOCK)
    sdist = dist_scale * dist
    scale = HEAD_DIM_AB ** -0.5
    for h in range(GROUP_HEADS):
        cs = slice(h * HEAD_DIM_AB, (h + 1) * HEAD_DIM_AB)
        k = jnp.concatenate([kp_ref[:, cs], kc_ref[:, cs]], axis=0)
        v = jnp.concatenate([vp_ref[:, cs], vc_ref[:, cs]], axis=0)
        s = lax.dot_general(q_ref[:, cs], k, (((1,), (1,)), ((), ())), preferred_element_type=F32) * scale
        s = jnp.where(valid, s - slopes[h] * sdist, -jnp.inf)
        m = jnp.max(s, axis=-1, keepdims=True)
        p = jnp.exp(s - m)
        denom = jnp.sum(p, axis=-1, keepdims=True)
        o = jnp.dot(p.astype(BF16), v, preferred_element_type=F32)
        o_ref[:, cs] = o / denom
        l_ref[:, cs] = jnp.broadcast_to(m + jnp.log(denom), (BLOCK, HEAD_DIM_AB))


def _attn_a_group(qkv, batch, seq, gi):
    window, r = A_PAIRS[gi]
    ncols = qkv.shape[1]
    cb = ncols // GROUP_W
    a_blocks = A_HEADS * HEAD_DIM_AB // GROUP_W
    lr = seq // r
    view = qkv.reshape(batch, lr, r * ncols)
    slopes = _alibi_slopes(A_HEADS)[gi * GROUP_HEADS:(gi + 1) * GROUP_HEADS]

    def spec(section, prev):
        def idx(b, j, i):
            return (b, jnp.maximum(i - 1, 0) if prev else i, j * cb + section * a_blocks + gi)
        return pl.BlockSpec((None, BLOCK, GROUP_W), idx)

    out_spec = pl.BlockSpec((None, BLOCK, GROUP_W), lambda b, j, i: (b, i, j))
    shape = jax.ShapeDtypeStruct((batch, lr, r * GROUP_W), F32)
    o, l = pl.pallas_call(
        functools.partial(_attn_a_kernel, slopes=slopes, dist_scale=float(r), max_dist=window // r),
        grid=(batch, r, lr // BLOCK),
        in_specs=[spec(0, False), spec(1, True), spec(1, False), spec(2, True), spec(2, False)],
        out_specs=[out_spec, out_spec],
        out_shape=[shape, shape],
        compiler_params=_params("parallel", "parallel", "arbitrary"),
        name=f"attn_a_r{r}",
    )(view, view, view, view, view)
    return o.reshape(batch * seq, GROUP_W), l.reshape(batch * seq, GROUP_W)


def _sb_kernel(q_ref, k_ref, v_ref, o_ref, *, tile):
    iq = pl.program_id(2)
    q = q_ref[...]
    scale = HEAD_DIM_AB ** -0.5
    row = lax.broadcasted_iota(jnp.int32, (tile, tile), 0)
    col = lax.broadcasted_iota(jnp.int32, (tile, tile), 1)
    later = (row > col).astype(BF16)

    def step(j, carry, acc, diagonal):
        start = pl.multiple_of(j * tile, tile)
        k = k_ref[pl.ds(start, tile), :]
        v = v_ref[pl.ds(start, tile), :]
        z = lax.dot_general(q, k, (((1,), (1,)), ((), ())), preferred_element_type=F32) * scale
        tail = jnp.log1p(jnp.exp(-jnp.abs(z)))
        log_keep = -(jnp.maximum(z, 0.0) + tail)
        log_hit = jnp.minimum(z, 0.0) - tail
        if diagonal:
            causal = col < row
            log_keep = jnp.where(causal, log_keep, 0.0)
        hi = log_keep.astype(BF16)
        lo = (log_keep - hi.astype(F32)).astype(BF16)
        after = (jnp.dot(hi, later, preferred_element_type=F32)
                 + jnp.dot(lo, later, preferred_element_type=F32) + carry)
        a = jnp.exp(log_hit + after)
        if diagonal:
            a = jnp.where(causal, a, 0.0)
        acc = acc + jnp.dot(a.astype(BF16), v, preferred_element_type=F32)
        carry = carry + jnp.sum(log_keep, axis=-1, keepdims=True)
        return carry, acc

    carry, acc = step(iq, jnp.zeros((tile, 1), F32), jnp.zeros((tile, HEAD_DIM_AB), F32), True)

    def body(n, state):
        return step(iq - n, state[0], state[1], False)

    carry, acc = lax.fori_loop(1, iq + 1, body, (carry, acc))
    o_ref[...] = acc.astype(o_ref.dtype)


def _stick_breaking(qkv, batch, seq, tile=BLOCK):
    ncols = qkv.shape[1]
    view = qkv.reshape(batch, seq, ncols)
    base = 3 * A_HEADS
    return pl.pallas_call(
        functools.partial(_sb_kernel, tile=tile),
        grid=(batch, GROUP_HEADS, seq // tile),
        in_specs=[
            pl.BlockSpec((None, tile, HEAD_DIM_AB), lambda b, h, i: (b, i, base + h)),
            pl.BlockSpec((None, seq, HEAD_DIM_AB), lambda b, h, i: (b, 0, base + GROUP_HEADS + h)),
            pl.BlockSpec((None, seq, HEAD_DIM_AB), lambda b, h, i: (b, 0, base + 2 * GROUP_HEADS + h)),
        ],
        out_specs=pl.BlockSpec((None, tile, HEAD_DIM_AB), lambda b, h, i: (b, i, h)),
        out_shape=jax.ShapeDtypeStruct((batch, seq, GROUP_W), BF16),
        compiler_params=_params("parallel", "parallel", "arbitrary"),
        name="stick_breaking",
    )(view, view, view)


def _combine_kernel(o1_ref, l1_ref, o2_ref, l2_ref, o3_ref, l3_ref, ob_ref, out_ref):
    l1, l2, l3 = l1_ref[...], l2_ref[...], l3_ref[...]
    m = jnp.maximum(l1, jnp.maximum(l2, l3))
    e1, e2, e3 = jnp.exp(l1 - m), jnp.exp(l2 - m), jnp.exp(l3 - m)
    num = e1 * o1_ref[...] + e2 * o2_ref[...] + e3 * o3_ref[...]
    out_ref[:, :GROUP_W] = (num / (e1 + e2 + e3)).astype(out_ref.dtype)
    out_ref[:, GROUP_W:] = ob_ref[...]


def _combine(groups, out_b, tm=512):
    t = out_b.shape[0]
    spec = pl.BlockSpec((tm, GROUP_W), lambda i: (i, 0))
    args = [a for pair in groups for a in pair] + [out_b]
    return pl.pallas_call(
        _combine_kernel,
        grid=(t // tm,),
        in_specs=[spec] * len(args),
        out_specs=pl.BlockSpec((tm, 2 * GROUP_W), lambda i: (i, 0)),
        out_shape=jax.ShapeDtypeStruct((t, 2 * GROUP_W), BF16),
        compiler_params=_params("parallel"),
        name="combine_ab",
    )(*args)


def _attn_c_kernel(sink_ref, q_ref, kp_ref, kc_ref, vp_ref, vc_ref, o_ref, *, slopes):
    i = pl.program_id(1)
    keys = 2 * BLOCK
    dist, valid = _band_masks((i - 1) * BLOCK, C_WINDOW - 1, 2 * keys)
    left = lax.broadcasted_iota(jnp.int32, (BLOCK, 2 * keys), 1) < keys
    low_in = lax.broadcasted_iota(jnp.int32, (keys, LANES), 1) < C_HEAD_DIM
    low_out = lax.broadcasted_iota(jnp.int32, (BLOCK, LANES), 1) < C_HEAD_DIM
    scale = C_HEAD_DIM ** -0.5
    group = C_Q_HEADS // C_KV_HEADS

    def block_diag(x, x_rolled, upper):
        src_low, src_high = (x_rolled, x) if upper else (x, x_rolled)
        zero = jnp.zeros_like(x)
        return jnp.concatenate([jnp.where(low_in, src_low, zero), jnp.where(low_in, zero, src_high)], axis=0)

    for c in range(C_KV_HEADS * C_HEAD_DIM // LANES):
        cs = slice(c * LANES, (c + 1) * LANES)
        kch = jnp.concatenate([kp_ref[:, cs], kc_ref[:, cs]], axis=0)
        vch = jnp.concatenate([vp_ref[:, cs], vc_ref[:, cs]], axis=0)
        kro = pltpu.roll(kch.astype(F32), C_HEAD_DIM, 1).astype(BF16)
        vro = pltpu.roll(vch.astype(F32), C_HEAD_DIM, 1).astype(BF16)
        for e in range(LANES // C_HEAD_DIM):
            kv = c * (LANES // C_HEAD_DIM) + e
            kb = block_diag(kch, kro, e == 1)
            vb = block_diag(vch, vro, e == 1)
            for p in range(group // 2):
                h0 = kv * group + 2 * p
                qs = slice(h0 * C_HEAD_DIM, (h0 + 2) * C_HEAD_DIM)
                s = lax.dot_general(q_ref[:, qs], kb, (((1,), (1,)), ((), ())), preferred_element_type=F32) * scale
                slope = jnp.where(left, slopes[h0], slopes[h0 + 1])
                s = jnp.where(valid, s - slope * (1.0 * dist), -jnp.inf)
                sink0, sink1 = sink_ref[h0], sink_ref[h0 + 1]
                m0 = jnp.maximum(jnp.max(s[:, :keys], axis=-1, keepdims=True), sink0)
                m1 = jnp.maximum(jnp.max(s[:, keys:], axis=-1, keepdims=True), sink1)
                pr = jnp.exp(s - jnp.where(left, m0, m1))
                d0 = jnp.sum(pr[:, :keys], axis=-1, keepdims=True) + jnp.exp(sink0 - m0)
                d1 = jnp.sum(pr[:, keys:], axis=-1, keepdims=True) + jnp.exp(sink1 - m1)
                o = jnp.dot(pr.astype(BF16), vb, preferred_element_type=F32)
                o_ref[:, qs] = (o / jnp.where(low_out, d0, d1)).astype(o_ref.dtype)


def _attn_c(qkv, sinks, batch, seq):
    ncols = qkv.shape[1]
    qw = C_Q_HEADS * C_HEAD_DIM
    kvw = C_KV_HEADS * C_HEAD_DIM
    view = qkv.reshape(batch, seq, ncols)
    kblk, vblk = qw // kvw, qw // kvw + 1

    def kv_spec(blk, prev):
        def idx(b, i):
            return (b, jnp.maximum(i - 1, 0) if prev else i, blk)
        return pl.BlockSpec((None, BLOCK, kvw), idx)

    out = pl.pallas_call(
        functools.partial(_attn_c_kernel, slopes=_alibi_slopes(C_Q_HEADS)),
        grid=(batch, seq // BLOCK),
        in_specs=[
            pl.BlockSpec(memory_space=pltpu.SMEM),
            pl.BlockSpec((None, BLOCK, qw), lambda b, i: (b, i, 0)),
            kv_spec(kblk, True), kv_spec(kblk, False), kv_spec(vblk, True), kv_spec(vblk, False),
        ],
        out_specs=pl.BlockSpec((None, BLOCK, qw), lambda b, i: (b, i, 0)),
        out_shape=jax.ShapeDtypeStruct((batch, seq, qw), BF16),
        compiler_params=_params("parallel", "arbitrary"),
        name="attn_c",
    )(sinks.astype(F32), view, view, view, view, view)
    return out.reshape(batch * seq, qw)


MM_TM = 1024
MM_TN = 512


def _ffn(x, norm, w_in, w_out):
    h = _rmsnorm(x, norm, BF16)
    hidden = _matmul_swiglu(h, w_in.astype(BF16), MM_TM, MM_TN)
    return _matmul_residual(hidden, w_out.astype(BF16), x, 0.5, MM_TM, MM_TN)


def _mixer_ab(x, norm, w_in, w_out, batch, seq):
    h = _rmsnorm(x, norm, BF16)
    qkv = _matmul(h, w_in.astype(BF16), MM_TM, MM_TN)
    groups = [_attn_a_group(qkv, batch, seq, gi) for gi in range(len(A_PAIRS))]
    out_b = _stick_breaking(qkv, batch, seq).reshape(batch * seq, GROUP_W)
    cat = _combine(groups, out_b)
    return _matmul_residual(cat, w_out.astype(BF16), x, 1.0, MM_TM, MM_TN)


def _mixer_c(x, norm, w_in, w_out, sinks, batch, seq):
    h = _rmsnorm(x, norm, BF16)
    qkv = _matmul(h, w_in.astype(BF16), MM_TM, MM_TN)
    o = _attn_c(qkv, sinks, batch, seq)
    return _matmul_residual(o, w_out.astype(BF16), x, 1.0, MM_TM, MM_TN)


def kernel(x, ffn1_norm, ffn1_w_in, ffn1_w_out, mix_norm, ab_w_in, ab_w_out, c_w_in, c_w_out, c_sinks,
           ffn2_norm, ffn2_w_in, ffn2_w_out, final_norm):
    batch, seq, d = x.shape
    depth = ffn1_norm.shape[0]
    x = x.reshape(batch * seq, d)
    for l in range(depth):
        x = _ffn(x, ffn1_norm[l], ffn1_w_in[l], ffn1_w_out[l])
        if l % 2 == 0:
            x = _mixer_ab(x, mix_norm[l], ab_w_in[l // 2], ab_w_out[l // 2], batch, seq)
        else:
            x = _mixer_c(x, mix_norm[l], c_w_in[l // 2], c_w_out[l // 2], c_sinks[l // 2], batch, seq)
        x = _ffn(x, ffn2_norm[l], ffn2_w_in[l], ffn2_w_out[l])
    return _rmsnorm(x, final_norm, F32).reshape(batch, seq, d)
```

```python
import functools

import numpy as np
import jax
import jax.numpy as jnp
from jax import lax
from jax.experimental import pallas as pl
from jax.experimental.pallas import tpu as pltpu

F32 = jnp.float32
BF16 = jnp.bfloat16

BLOCK = 128
RMS_EPS = 1e-6
HEAD_DIM_AB = 128
A_PAIRS = ((128, 1), (512, 4), (2048, 16))
GROUP_HEADS = 8
A_HEADS = GROUP_HEADS * len(A_PAIRS)
GROUP_W = GROUP_HEADS * HEAD_DIM_AB
C_HEAD_DIM = 64
C_Q_HEADS = 64
C_KV_HEADS = 8
C_WINDOW = 128

LANES = 128
VMEM_LIMIT_BYTES = 56 * 1024 * 1024


def _alibi_slopes(n):
    return [float(v) for v in np.asarray(2.0 ** (-8.0 * np.arange(1, n + 1) / n), dtype=np.float32)]


def _params(*sem):
    return pltpu.CompilerParams(dimension_semantics=sem, vmem_limit_bytes=VMEM_LIMIT_BYTES)


def _rmsnorm_kernel(x_ref, g_ref, o_ref):
    x = x_ref[...]
    ms = jnp.mean(x * x, axis=-1, keepdims=True)
    o_ref[...] = (x * lax.rsqrt(ms + RMS_EPS) * g_ref[...]).astype(o_ref.dtype)


def _rmsnorm(x, g, out_dtype, tm=256):
    t, d = x.shape
    return pl.pallas_call(
        _rmsnorm_kernel,
        grid=(t // tm,),
        in_specs=[pl.BlockSpec((tm, d), lambda i: (i, 0)), pl.BlockSpec((1, d), lambda i: (0, 0))],
        out_specs=pl.BlockSpec((tm, d), lambda i: (i, 0)),
        out_shape=jax.ShapeDtypeStruct((t, d), out_dtype),
        compiler_params=_params("parallel"),
        name="rmsnorm",
    )(x, g.reshape(1, d).astype(F32))


def _cast_weights(w_refs, wb_refs):
    @pl.when(pl.program_id(1) == 0)
    def _():
        for w_ref, wb_ref in zip(w_refs, wb_refs):
            wb_ref[...] = w_ref[...].astype(BF16)


def _mm_kernel(a_ref, w_ref, o_ref, wb_ref):
    _cast_weights([w_ref], [wb_ref])
    o_ref[...] = jnp.dot(a_ref[...], wb_ref[...], preferred_element_type=F32).astype(o_ref.dtype)


def _mm_swiglu_kernel(a_ref, wg_ref, wu_ref, o_ref, wgb_ref, wub_ref):
    _cast_weights([wg_ref, wu_ref], [wgb_ref, wub_ref])
    a = a_ref[...]
    g = jnp.dot(a, wgb_ref[...], preferred_element_type=F32)
    u = jnp.dot(a, wub_ref[...], preferred_element_type=F32)
    o_ref[...] = (g * jax.nn.sigmoid(g) * u).astype(o_ref.dtype)


def _mm_residual_kernel(a_ref, w_ref, x_ref, o_ref, wb_ref, *, scale):
    _cast_weights([w_ref], [wb_ref])
    acc = jnp.dot(a_ref[...], wb_ref[...], preferred_element_type=F32)
    o_ref[...] = x_ref[...] + scale * acc


def _weight_spec(layer, k, tn, col_block_offset=0):
    return pl.BlockSpec((None, k, tn), lambda j, i: (layer, 0, j + col_block_offset), pipeline_mode=pl.Buffered(1))


def _matmul(a, w, layer, tm, tn):
    t, k = a.shape
    n = w.shape[2]
    return pl.pallas_call(
        _mm_kernel,
        grid=(n // tn, t // tm),
        in_specs=[pl.BlockSpec((tm, k), lambda j, i: (i, 0)), _weight_spec(layer, k, tn)],
        out_specs=pl.BlockSpec((tm, tn), lambda j, i: (i, j)),
        out_shape=jax.ShapeDtypeStruct((t, n), BF16),
        scratch_shapes=[pltpu.VMEM((k, tn), BF16)],
        compiler_params=_params("arbitrary", "arbitrary"),
        name="matmul",
    )(a, w)


def _matmul_swiglu(a, w_in, layer, tm, tn):
    t, k = a.shape
    f = w_in.shape[2] // 2
    nj = f // tn
    return pl.pallas_call(
        _mm_swiglu_kernel,
        grid=(nj, t // tm),
        in_specs=[pl.BlockSpec((tm, k), lambda j, i: (i, 0)), _weight_spec(layer, k, tn), _weight_spec(layer, k, tn, nj)],
        out_specs=pl.BlockSpec((tm, tn), lambda j, i: (i, j)),
        out_shape=jax.ShapeDtypeStruct((t, f), BF16),
        scratch_shapes=[pltpu.VMEM((k, tn), BF16), pltpu.VMEM((k, tn), BF16)],
        compiler_params=_params("arbitrary", "arbitrary"),
        name="matmul_swiglu",
    )(a, w_in, w_in)


def _matmul_residual(a, w, layer, x, scale, tm, tn):
    t, k = a.shape
    n = w.shape[2]
    return pl.pallas_call(
        functools.partial(_mm_residual_kernel, scale=scale),
        grid=(n // tn, t // tm),
        in_specs=[
            pl.BlockSpec((tm, k), lambda j, i: (i, 0)),
            _weight_spec(layer, k, tn),
            pl.BlockSpec((tm, tn), lambda j, i: (i, j)),
        ],
        out_specs=pl.BlockSpec((tm, tn), lambda j, i: (i, j)),
        out_shape=jax.ShapeDtypeStruct((t, n), F32),
        scratch_shapes=[pltpu.VMEM((k, tn), BF16)],
        input_output_aliases={2: 0},
        compiler_params=_params("arbitrary", "arbitrary"),
        name="matmul_residual",
    )(a, w, x)


def _band_masks(first_block_offset, max_dist, width):
    qi = lax.broadcasted_iota(jnp.int32, (BLOCK, width), 0)
    si = lax.broadcasted_iota(jnp.int32, (BLOCK, width), 1) & (2 * BLOCK - 1)
    dist = qi - si + BLOCK
    valid = (dist >= 0) & (dist <= max_dist) & (si + first_block_offset >= 0)
    return dist.astype(F32), valid


def _attn_a_kernel(q_ref, kp_ref, kc_ref, vp_ref, vc_ref, o_ref, l_ref, *, slopes, dist_scale, max_dist):
    i = pl.program_id(1)
    dist, valid = _band_masks((i - 1) * BLOCK, max_dist, 2 * BLOCK)
    sdist = dist_scale * dist
    scale = HEAD_DIM_AB ** -0.5
    for h in range(GROUP_HEADS):
        cs = slice(h * HEAD_DIM_AB, (h + 1) * HEAD_DIM_AB)
        k = jnp.concatenate([kp_ref[:, cs], kc_ref[:, cs]], axis=0)
        v = jnp.concatenate([vp_ref[:, cs], vc_ref[:, cs]], axis=0)
        s = lax.dot_general(q_ref[:, cs], k, (((1,), (1,)), ((), ())), preferred_element_type=F32) * scale
        s = jnp.where(valid, s - slopes[h] * sdist, -jnp.inf)
        m = jnp.max(s, axis=-1, keepdims=True)
        p = jnp.exp(s - m)
        denom = jnp.sum(p, axis=-1, keepdims=True)
        o = jnp.dot(p.astype(BF16), v, preferred_element_type=F32)
        o_ref[:, cs] = o / denom
        l_ref[:, cs] = jnp.broadcast_to(m + jnp.log(denom), (BLOCK, HEAD_DIM_AB))


def _attn_a_group(qkv, batch, seq, gi):
    window, r = A_PAIRS[gi]
    a_blocks = A_HEADS * HEAD_DIM_AB // GROUP_W
    lr = seq // r
    slopes = _alibi_slopes(A_HEADS)[gi * GROUP_HEADS:(gi + 1) * GROUP_HEADS]
    if r == 1:
        src = qkv.reshape(batch, seq, qkv.shape[1])
        col_blocks = [section * a_blocks + gi for section in range(3)]
    else:
        cols = [qkv[:, (section * a_blocks + gi) * GROUP_W:(section * a_blocks + gi + 1) * GROUP_W] for section in range(3)]
        src = jnp.concatenate(cols, axis=1).reshape(batch, lr, r, 3 * GROUP_W)
        src = src.swapaxes(1, 2).reshape(batch * r, lr, 3 * GROUP_W)
        col_blocks = [0, 1, 2]

    def spec(section, prev):
        def idx(n, i):
            return (n, jnp.maximum(i - 1, 0) if prev else i, col_blocks[section])
        return pl.BlockSpec((None, BLOCK, GROUP_W), idx)

    out_spec = pl.BlockSpec((None, BLOCK, GROUP_W), lambda n, i: (n, i, 0))
    shape = jax.ShapeDtypeStruct((batch * r, lr, GROUP_W), F32)
    o, l = pl.pallas_call(
        functools.partial(_attn_a_kernel, slopes=slopes, dist_scale=float(r), max_dist=window // r),
        grid=(batch * r, lr // BLOCK),
        in_specs=[spec(0, False), spec(1, True), spec(1, False), spec(2, True), spec(2, False)],
        out_specs=[out_spec, out_spec],
        out_shape=[shape, shape],
        compiler_params=_params("parallel", "arbitrary"),
        name=f"attn_a_r{r}",
    )(src, src, src, src, src)

    def undilate(t):
        return t.reshape(batch, r, lr, GROUP_W).swapaxes(1, 2).reshape(batch * seq, GROUP_W)

    return undilate(o), undilate(l)


def _sb_kernel(q_ref, k_ref, v_ref, o_ref, carry_ref, acc_ref, *, tile):
    iq = pl.program_id(1)
    scale = HEAD_DIM_AB ** -0.5
    row = lax.broadcasted_iota(jnp.int32, (tile, tile), 0)
    col = lax.broadcasted_iota(jnp.int32, (tile, tile), 1)
    suffix = jnp.concatenate([(row > col).astype(BF16), jnp.ones((tile, tile), BF16)], axis=1)
    suffix = jnp.concatenate([suffix, suffix], axis=0)
    causal = col < row
    heads = range(GROUP_HEADS)
    cols = [slice(h * HEAD_DIM_AB, (h + 1) * HEAD_DIM_AB) for h in heads]

    def step(j, diagonal):
        start = pl.multiple_of(j * tile, tile)
        zs = [lax.dot_general(q_ref[:, cs], k_ref[pl.ds(start, tile), cs], (((1,), (1,)), ((), ())),
                              preferred_element_type=F32) * scale for cs in cols]
        log_hits, splits = [], []
        for z in zs:
            tail = jnp.log(1.0 + jnp.exp(-jnp.abs(z)))
            log_keep = -(jnp.maximum(z, 0.0) + tail)
            log_hits.append(jnp.minimum(z, 0.0) - tail)
            if diagonal:
                log_keep = jnp.where(causal, log_keep, 0.0)
            hi = log_keep.astype(BF16)
            lo = (log_keep - hi.astype(F32)).astype(BF16)
            splits.append(jnp.concatenate([hi, lo], axis=1))
        sums = jnp.dot(jnp.concatenate(splits, axis=0), suffix, preferred_element_type=F32)
        for h, cs in zip(heads, cols):
            after = sums[h * tile:(h + 1) * tile, :tile]
            rowsum = sums[h * tile:(h + 1) * tile, tile:]
            if not diagonal:
                after = after + carry_ref[:, cs]
            a = jnp.exp(log_hits[h] + after)
            if diagonal:
                a = jnp.where(causal, a, 0.0)
            pv = jnp.dot(a.astype(BF16), v_ref[pl.ds(start, tile), cs], preferred_element_type=F32)
            if diagonal:
                acc_ref[:, cs] = pv
                carry_ref[:, cs] = rowsum
            else:
                acc_ref[:, cs] += pv
                carry_ref[:, cs] += rowsum

    step(iq, True)

    def body(n, state):
        step(iq - n, False)
        return state

    lax.fori_loop(1, iq + 1, body, 0)
    o_ref[...] = acc_ref[...].astype(o_ref.dtype)


def _stick_breaking(qkv, batch, seq, tile=BLOCK):
    ncols = qkv.shape[1]
    view = qkv.reshape(batch, seq, ncols)
    base = 3 * A_HEADS * HEAD_DIM_AB // GROUP_W
    resident = dict(pipeline_mode=pl.Buffered(1))
    out = pl.pallas_call(
        functools.partial(_sb_kernel, tile=tile),
        grid=(batch, seq // tile),
        in_specs=[
            pl.BlockSpec((None, tile, GROUP_W), lambda b, i: (b, i, base)),
            pl.BlockSpec((None, seq, GROUP_W), lambda b, i: (b, 0, base + 1), **resident),
            pl.BlockSpec((None, seq, GROUP_W), lambda b, i: (b, 0, base + 2), **resident),
        ],
        out_specs=pl.BlockSpec((None, tile, GROUP_W), lambda b, i: (b, i, 0)),
        out_shape=jax.ShapeDtypeStruct((batch, seq, GROUP_W), BF16),
        scratch_shapes=[pltpu.VMEM((tile, GROUP_W), F32), pltpu.VMEM((tile, GROUP_W), F32)],
        compiler_params=_params("parallel", "arbitrary"),
        name="stick_breaking",
    )(view, view, view)
    return out.reshape(batch * seq, GROUP_W)


def _combine_kernel(o1_ref, l1_ref, o2_ref, l2_ref, o3_ref, l3_ref, ob_ref, out_ref):
    l1, l2, l3 = l1_ref[...], l2_ref[...], l3_ref[...]
    m = jnp.maximum(l1, jnp.maximum(l2, l3))
    e1, e2, e3 = jnp.exp(l1 - m), jnp.exp(l2 - m), jnp.exp(l3 - m)
    num = e1 * o1_ref[...] + e2 * o2_ref[...] + e3 * o3_ref[...]
    out_ref[:, :GROUP_W] = (num / (e1 + e2 + e3)).astype(out_ref.dtype)
    out_ref[:, GROUP_W:] = ob_ref[...]


def _combine(groups, out_b, tm=512):
    t = out_b.shape[0]
    spec = pl.BlockSpec((tm, GROUP_W), lambda i: (i, 0))
    args = [a for pair in groups for a in pair] + [out_b]
    return pl.pallas_call(
        _combine_kernel,
        grid=(t // tm,),
        in_specs=[spec] * len(args),
        out_specs=pl.BlockSpec((tm, 2 * GROUP_W), lambda i: (i, 0)),
        out_shape=jax.ShapeDtypeStruct((t, 2 * GROUP_W), BF16),
        compiler_params=_params("parallel"),
        name="combine_ab",
    )(*args)


def _attn_c_kernel(sink_ref, q_ref, kp_ref, kc_ref, vp_ref, vc_ref, o_ref, *, slopes):
    i = pl.program_id(1)
    keys = 2 * BLOCK
    dist, valid = _band_masks((i - 1) * BLOCK, C_WINDOW - 1, 2 * keys)
    left = lax.broadcasted_iota(jnp.int32, (BLOCK, 2 * keys), 1) < keys
    low_in = lax.broadcasted_iota(jnp.int32, (keys, LANES), 1) < C_HEAD_DIM
    low_out = lax.broadcasted_iota(jnp.int32, (BLOCK, LANES), 1) < C_HEAD_DIM
    scale = C_HEAD_DIM ** -0.5
    group = C_Q_HEADS // C_KV_HEADS

    def block_diag(x, x_rolled, upper):
        src_low, src_high = (x_rolled, x) if upper else (x, x_rolled)
        zero = jnp.zeros_like(x)
        return jnp.concatenate([jnp.where(low_in, src_low, zero), jnp.where(low_in, zero, src_high)], axis=0)

    for c in range(C_KV_HEADS * C_HEAD_DIM // LANES):
        cs = slice(c * LANES, (c + 1) * LANES)
        kch = jnp.concatenate([kp_ref[:, cs], kc_ref[:, cs]], axis=0)
        vch = jnp.concatenate([vp_ref[:, cs], vc_ref[:, cs]], axis=0)
        kro = pltpu.roll(kch.astype(F32), C_HEAD_DIM, 1).astype(BF16)
        vro = pltpu.roll(vch.astype(F32), C_HEAD_DIM, 1).astype(BF16)
        for e in range(LANES // C_HEAD_DIM):
            kv = c * (LANES // C_HEAD_DIM) + e
            kb = block_diag(kch, kro, e == 1)
            vb = block_diag(vch, vro, e == 1)
            for p in range(group // 2):
                h0 = kv * group + 2 * p
                qs = slice(h0 * C_HEAD_DIM, (h0 + 2) * C_HEAD_DIM)
                s = lax.dot_general(q_ref[:, qs], kb, (((1,), (1,)), ((), ())), preferred_element_type=F32) * scale
                slope = jnp.where(left, slopes[h0], slopes[h0 + 1])
                s = jnp.where(valid, s - slope * (1.0 * dist), -jnp.inf)
                sink0, sink1 = sink_ref[h0], sink_ref[h0 + 1]
                m0 = jnp.maximum(jnp.max(s[:, :keys], axis=-1, keepdims=True), sink0)
                m1 = jnp.maximum(jnp.max(s[:, keys:], axis=-1, keepdims=True), sink1)
                pr = jnp.exp(s - jnp.where(left, m0, m1))
                d0 = jnp.sum(pr[:, :keys], axis=-1, keepdims=True) + jnp.exp(sink0 - m0)
                d1 = jnp.sum(pr[:, keys:], axis=-1, keepdims=True) + jnp.exp(sink1 - m1)
                o = jnp.dot(pr.astype(BF16), vb, preferred_element_type=F32)
                o_ref[:, qs] = (o / jnp.where(low_out, d0, d1)).astype(o_ref.dtype)


def _attn_c(qkv, sinks, batch, seq):
    ncols = qkv.shape[1]
    qw = C_Q_HEADS * C_HEAD_DIM
    kvw = C_KV_HEADS * C_HEAD_DIM
    view = qkv.reshape(batch, seq, ncols)
    kblk, vblk = qw // kvw, qw // kvw + 1

    def kv_spec(blk, prev):
        def idx(b, i):
            return (b, jnp.maximum(i - 1, 0) if prev else i, blk)
        return pl.BlockSpec((None, BLOCK, kvw), idx)

    out = pl.pallas_call(
        functools.partial(_attn_c_kernel, slopes=_alibi_slopes(C_Q_HEADS)),
        grid=(batch, seq // BLOCK),
        in_specs=[
            pl.BlockSpec(memory_space=pltpu.SMEM),
            pl.BlockSpec((None, BLOCK, qw), lambda b, i: (b, i, 0)),
            kv_spec(kblk, True), kv_spec(kblk, False), kv_spec(vblk, True), kv_spec(vblk, False),
        ],
        out_specs=pl.BlockSpec((None, BLOCK, qw), lambda b, i: (b, i, 0)),
        out_shape=jax.ShapeDtypeStruct((batch, seq, qw), BF16),
        compiler_params=_params("parallel", "arbitrary"),
        name="attn_c",
    )(sinks.astype(F32), view, view, view, view, view)
    return out.reshape(batch * seq, qw)


MM_TN = 512


def _token_tile(k):
    return 1024 if k <= 4096 else 512


def _ffn(x, norm, w_in, w_out, layer):
    h = _rmsnorm(x, norm[layer], BF16)
    hidden = _matmul_swiglu(h, w_in, layer, _token_tile(h.shape[1]), MM_TN)
    return _matmul_residual(hidden, w_out, layer, x, 0.5, _token_tile(hidden.shape[1]), MM_TN)


def _mixer_ab(x, norm, w_in, w_out, layer, batch, seq):
    h = _rmsnorm(x, norm, BF16)
    qkv = _matmul(h, w_in, layer, _token_tile(h.shape[1]), MM_TN)
    groups = [_attn_a_group(qkv, batch, seq, gi) for gi in range(len(A_PAIRS))]
    out_b = _stick_breaking(qkv, batch, seq)
    cat = _combine(groups, out_b)
    return _matmul_residual(cat, w_out, layer, x, 1.0, _token_tile(cat.shape[1]), MM_TN)


def _mixer_c(x, norm, w_in, w_out, sinks, layer, batch, seq):
    h = _rmsnorm(x, norm, BF16)
    qkv = _matmul(h, w_in, layer, _token_tile(h.shape[1]), MM_TN)
    o = _attn_c(qkv, sinks, batch, seq)
    return _matmul_residual(o, w_out, layer, x, 1.0, _token_tile(o.shape[1]), MM_TN)


def kernel(x, ffn1_norm, ffn1_w_in, ffn1_w_out, mix_norm, ab_w_in, ab_w_out, c_w_in, c_w_out, c_sinks,
           ffn2_norm, ffn2_w_in, ffn2_w_out, final_norm):
    batch, seq, d = x.shape
    depth = ffn1_norm.shape[0]
    x = x.reshape(batch * seq, d)
    for l in range(depth):
        x = _ffn(x, ffn1_norm, ffn1_w_in, ffn1_w_out, l)
        if l % 2 == 0:
            x = _mixer_ab(x, mix_norm[l], ab_w_in, ab_w_out, l // 2, batch, seq)
        else:
            x = _mixer_c(x, mix_norm[l], c_w_in, c_w_out, c_sinks[l // 2], l // 2, batch, seq)
        x = _ffn(x, ffn2_norm, ffn2_w_in, ffn2_w_out, l)
    return _rmsnorm(x, final_norm, F32).reshape(batch, seq, d)
```

```python
import functools

import numpy as np
import jax
import jax.numpy as jnp
from jax import lax
from jax.experimental import pallas as pl
from jax.experimental.pallas import tpu as pltpu

F32 = jnp.float32
BF16 = jnp.bfloat16

BLOCK = 128
RMS_EPS = 1e-6
HEAD_DIM_AB = 128
A_PAIRS = ((128, 1), (512, 4), (2048, 16))
GROUP_HEADS = 8
A_HEADS = GROUP_HEADS * len(A_PAIRS)
GROUP_W = GROUP_HEADS * HEAD_DIM_AB
C_HEAD_DIM = 64
C_Q_HEADS = 64
C_KV_HEADS = 8
C_WINDOW = 128
EXP_UNDERFLOW = -106.0

LANES = 128
VMEM_LIMIT_BYTES = 56 * 1024 * 1024


def _alibi_slopes(n):
    return [float(v) for v in np.asarray(2.0 ** (-8.0 * np.arange(1, n + 1) / n), dtype=np.float32)]


def _params(*sem):
    return pltpu.CompilerParams(dimension_semantics=sem, vmem_limit_bytes=VMEM_LIMIT_BYTES)


def _rmsnorm_kernel(x_ref, g_ref, o_ref):
    x = x_ref[...]
    ms = jnp.mean(x * x, axis=-1, keepdims=True)
    o_ref[...] = (x * lax.rsqrt(ms + RMS_EPS) * g_ref[...]).astype(o_ref.dtype)


def _rmsnorm(x, g, out_dtype, tm=256):
    t, d = x.shape
    return pl.pallas_call(
        _rmsnorm_kernel,
        grid=(t // tm,),
        in_specs=[pl.BlockSpec((tm, d), lambda i: (i, 0)), pl.BlockSpec((1, d), lambda i: (0, 0))],
        out_specs=pl.BlockSpec((tm, d), lambda i: (i, 0)),
        out_shape=jax.ShapeDtypeStruct((t, d), out_dtype),
        compiler_params=_params("parallel"),
        name="rmsnorm",
    )(x, g.reshape(1, d).astype(F32))


def _cast_weights(w_refs, wb_refs):
    @pl.when(pl.program_id(1) == 0)
    def _():
        for w_ref, wb_ref in zip(w_refs, wb_refs):
            wb_ref[...] = w_ref[...].astype(BF16)


def _mm_kernel(a_ref, w_ref, o_ref, wb_ref):
    _cast_weights([w_ref], [wb_ref])
    o_ref[...] = jnp.dot(a_ref[...], wb_ref[...], preferred_element_type=F32).astype(o_ref.dtype)


def _mm_swiglu_kernel(a_ref, wg_ref, wu_ref, o_ref, wgb_ref, wub_ref):
    _cast_weights([wg_ref, wu_ref], [wgb_ref, wub_ref])
    a = a_ref[...]
    g = jnp.dot(a, wgb_ref[...], preferred_element_type=F32)
    u = jnp.dot(a, wub_ref[...], preferred_element_type=F32)
    o_ref[...] = (g * jax.nn.sigmoid(g) * u).astype(o_ref.dtype)


def _mm_residual_kernel(a_ref, w_ref, x_ref, o_ref, wb_ref, *, scale):
    _cast_weights([w_ref], [wb_ref])
    acc = jnp.dot(a_ref[...], wb_ref[...], preferred_element_type=F32)
    o_ref[...] = x_ref[...] + scale * acc


def _weight_spec(layer, k, tn, col_block_offset=0):
    return pl.BlockSpec((None, k, tn), lambda j, i: (layer, 0, j + col_block_offset), pipeline_mode=pl.Buffered(1))


def _matmul(a, w, layer, tm, tn):
    t, k = a.shape
    n = w.shape[2]
    return pl.pallas_call(
        _mm_kernel,
        grid=(n // tn, t // tm),
        in_specs=[pl.BlockSpec((tm, k), lambda j, i: (i, 0)), _weight_spec(layer, k, tn)],
        out_specs=pl.BlockSpec((tm, tn), lambda j, i: (i, j)),
        out_shape=jax.ShapeDtypeStruct((t, n), BF16),
        scratch_shapes=[pltpu.VMEM((k, tn), BF16)],
        compiler_params=_params("arbitrary", "arbitrary"),
        name="matmul",
    )(a, w)


def _matmul_swiglu(a, w_in, layer, tm, tn):
    t, k = a.shape
    f = w_in.shape[2] // 2
    nj = f // tn
    return pl.pallas_call(
        _mm_swiglu_kernel,
        grid=(nj, t // tm),
        in_specs=[pl.BlockSpec((tm, k), lambda j, i: (i, 0)), _weight_spec(layer, k, tn), _weight_spec(layer, k, tn, nj)],
        out_specs=pl.BlockSpec((tm, tn), lambda j, i: (i, j)),
        out_shape=jax.ShapeDtypeStruct((t, f), BF16),
        scratch_shapes=[pltpu.VMEM((k, tn), BF16), pltpu.VMEM((k, tn), BF16)],
        compiler_params=_params("arbitrary", "arbitrary"),
        name="matmul_swiglu",
    )(a, w_in, w_in)


def _matmul_residual(a, w, layer, x, scale, tm, tn):
    t, k = a.shape
    n = w.shape[2]
    return pl.pallas_call(
        functools.partial(_mm_residual_kernel, scale=scale),
        grid=(n // tn, t // tm),
        in_specs=[
            pl.BlockSpec((tm, k), lambda j, i: (i, 0)),
            _weight_spec(layer, k, tn),
            pl.BlockSpec((tm, tn), lambda j, i: (i, j)),
        ],
        out_specs=pl.BlockSpec((tm, tn), lambda j, i: (i, j)),
        out_shape=jax.ShapeDtypeStruct((t, n), F32),
        scratch_shapes=[pltpu.VMEM((k, tn), BF16)],
        input_output_aliases={2: 0},
        compiler_params=_params("arbitrary", "arbitrary"),
        name="matmul_residual",
    )(a, w, x)


def _band_masks(first_block_offset, max_dist, width):
    qi = lax.broadcasted_iota(jnp.int32, (BLOCK, width), 0)
    si = lax.broadcasted_iota(jnp.int32, (BLOCK, width), 1) & (2 * BLOCK - 1)
    dist = qi - si + BLOCK
    valid = (dist >= 0) & (dist <= max_dist) & (si + first_block_offset >= 0)
    return dist.astype(F32), valid


def _attn_a_kernel(q_ref, kp_ref, kc_ref, vp_ref, vc_ref, o_ref, l_ref, *, slopes, dist_scale, max_dist):
    i = pl.program_id(1)
    dist, valid = _band_masks((i - 1) * BLOCK, max_dist, 2 * BLOCK)
    neg_dist = jnp.where(valid, -(dist_scale * dist), -jnp.inf)
    scale = HEAD_DIM_AB ** -0.5
    cols = [slice(h * HEAD_DIM_AB, (h + 1) * HEAD_DIM_AB) for h in range(GROUP_HEADS)]
    scores = [lax.dot_general(q_ref[:, cs], jnp.concatenate([kp_ref[:, cs], kc_ref[:, cs]], axis=0),
                              (((1,), (1,)), ((), ())), preferred_element_type=F32) for cs in cols]
    probs, stats = [], []
    for h, s in enumerate(scores):
        s = s * scale + slopes[h] * neg_dist
        m = jnp.max(s, axis=-1, keepdims=True)
        p = jnp.exp(s - m)
        stats.append((m, jnp.sum(p, axis=-1, keepdims=True)))
        probs.append(p.astype(BF16))
    for cs, p, (m, denom) in zip(cols, probs, stats):
        v = jnp.concatenate([vp_ref[:, cs], vc_ref[:, cs]], axis=0)
        o_ref[:, cs] = jnp.dot(p, v, preferred_element_type=F32) / denom
        l_ref[:, cs] = jnp.broadcast_to(m + jnp.log(denom), (BLOCK, HEAD_DIM_AB))


def _attn_a_group(qkv, batch, seq, gi):
    window, r = A_PAIRS[gi]
    a_blocks = A_HEADS * HEAD_DIM_AB // GROUP_W
    lr = seq // r
    slopes = _alibi_slopes(A_HEADS)[gi * GROUP_HEADS:(gi + 1) * GROUP_HEADS]
    if r == 1:
        src = qkv.reshape(batch, seq, qkv.shape[1])
        col_blocks = [section * a_blocks + gi for section in range(3)]
    else:
        cols = [qkv[:, (section * a_blocks + gi) * GROUP_W:(section * a_blocks + gi + 1) * GROUP_W] for section in range(3)]
        src = jnp.concatenate(cols, axis=1).reshape(batch, lr, r, 3 * GROUP_W)
        src = src.swapaxes(1, 2).reshape(batch * r, lr, 3 * GROUP_W)
        col_blocks = [0, 1, 2]

    def spec(section, prev):
        def idx(n, i):
            return (n, jnp.maximum(i - 1, 0) if prev else i, col_blocks[section])
        return pl.BlockSpec((None, BLOCK, GROUP_W), idx)

    out_spec = pl.BlockSpec((None, BLOCK, GROUP_W), lambda n, i: (n, i, 0))
    shape = jax.ShapeDtypeStruct((batch * r, lr, GROUP_W), F32)
    o, l = pl.pallas_call(
        functools.partial(_attn_a_kernel, slopes=slopes, dist_scale=float(r), max_dist=window // r),
        grid=(batch * r, lr // BLOCK),
        in_specs=[spec(0, False), spec(1, True), spec(1, False), spec(2, True), spec(2, False)],
        out_specs=[out_spec, out_spec],
        out_shape=[shape, shape],
        compiler_params=_params("parallel", "arbitrary"),
        name=f"attn_a_r{r}",
    )(src, src, src, src, src)

    def undilate(t):
        return t.reshape(batch, r, lr, GROUP_W).swapaxes(1, 2).reshape(batch * seq, GROUP_W)

    return undilate(o), undilate(l)


def _sb_kernel(q_ref, k_ref, v_ref, o_ref, carry_ref, acc_ref, *, tile):
    iq = pl.program_id(1)
    scale = HEAD_DIM_AB ** -0.5
    row = lax.broadcasted_iota(jnp.int32, (tile, tile), 0)
    col = lax.broadcasted_iota(jnp.int32, (tile, tile), 1)
    suffix = jnp.concatenate([(row > col).astype(BF16), jnp.ones((tile, tile), BF16)], axis=1)
    suffix = jnp.concatenate([suffix, suffix], axis=0)
    causal = col < row
    heads = range(GROUP_HEADS)
    cols = [slice(h * HEAD_DIM_AB, (h + 1) * HEAD_DIM_AB) for h in heads]

    def step(j, diagonal):
        start = pl.multiple_of(j * tile, tile)
        zs = [lax.dot_general(q_ref[:, cs], k_ref[pl.ds(start, tile), cs], (((1,), (1,)), ((), ())),
                              preferred_element_type=F32) * scale for cs in cols]
        log_hits, splits = [], []
        for z in zs:
            tail = jnp.log(1.0 + jnp.exp(-jnp.abs(z)))
            log_keep = -(jnp.maximum(z, 0.0) + tail)
            log_hits.append(jnp.minimum(z, 0.0) - tail)
            if diagonal:
                log_keep = jnp.where(causal, log_keep, 0.0)
            hi = log_keep.astype(BF16)
            lo = (log_keep - hi.astype(F32)).astype(BF16)
            splits.append(jnp.concatenate([hi, lo], axis=1))
        sums = jnp.dot(jnp.concatenate(splits, axis=0), suffix, preferred_element_type=F32)
        for h, cs in zip(heads, cols):
            after = sums[h * tile:(h + 1) * tile, :tile]
            rowsum = sums[h * tile:(h + 1) * tile, tile:]
            if not diagonal:
                after = after + carry_ref[:, cs]
            a = jnp.exp(log_hits[h] + after)
            if diagonal:
                a = jnp.where(causal, a, 0.0)
            pv = jnp.dot(a.astype(BF16), v_ref[pl.ds(start, tile), cs], preferred_element_type=F32)
            if diagonal:
                acc_ref[:, cs] = pv
                carry_ref[:, cs] = rowsum
            else:
                acc_ref[:, cs] += pv
                carry_ref[:, cs] += rowsum

    def any_row_alive():
        c = carry_ref[:, cols[0]]
        for cs in cols[1:]:
            c = jnp.maximum(c, carry_ref[:, cs])
        return jnp.max(c) >= EXP_UNDERFLOW

    step(iq, True)

    def cond(state):
        return jnp.logical_and(state[0] <= iq, state[1])

    def body(state):
        step(iq - state[0], False)
        return state[0] + 1, any_row_alive()

    lax.while_loop(cond, body, (jnp.int32(1), any_row_alive()))
    o_ref[...] = acc_ref[...].astype(o_ref.dtype)


def _stick_breaking(qkv, batch, seq, tile=BLOCK):
    ncols = qkv.shape[1]
    view = qkv.reshape(batch, seq, ncols)
    base = 3 * A_HEADS * HEAD_DIM_AB // GROUP_W
    resident = dict(pipeline_mode=pl.Buffered(1))
    out = pl.pallas_call(
        functools.partial(_sb_kernel, tile=tile),
        grid=(batch, seq // tile),
        in_specs=[
            pl.BlockSpec((None, tile, GROUP_W), lambda b, i: (b, i, base)),
            pl.BlockSpec((None, seq, GROUP_W), lambda b, i: (b, 0, base + 1), **resident),
            pl.BlockSpec((None, seq, GROUP_W), lambda b, i: (b, 0, base + 2), **resident),
        ],
        out_specs=pl.BlockSpec((None, tile, GROUP_W), lambda b, i: (b, i, 0)),
        out_shape=jax.ShapeDtypeStruct((batch, seq, GROUP_W), BF16),
        scratch_shapes=[pltpu.VMEM((tile, GROUP_W), F32), pltpu.VMEM((tile, GROUP_W), F32)],
        compiler_params=_params("parallel", "arbitrary"),
        name="stick_breaking",
    )(view, view, view)
    return out.reshape(batch * seq, GROUP_W)


def _combine_kernel(o1_ref, l1_ref, o2_ref, l2_ref, o3_ref, l3_ref, ob_ref, out_ref):
    l1, l2, l3 = l1_ref[...], l2_ref[...], l3_ref[...]
    m = jnp.maximum(l1, jnp.maximum(l2, l3))
    e1, e2, e3 = jnp.exp(l1 - m), jnp.exp(l2 - m), jnp.exp(l3 - m)
    num = e1 * o1_ref[...] + e2 * o2_ref[...] + e3 * o3_ref[...]
    out_ref[:, :GROUP_W] = (num / (e1 + e2 + e3)).astype(out_ref.dtype)
    out_ref[:, GROUP_W:] = ob_ref[...]


def _combine(groups, out_b, tm=512):
    t = out_b.shape[0]
    spec = pl.BlockSpec((tm, GROUP_W), lambda i: (i, 0))
    args = [a for pair in groups for a in pair] + [out_b]
    return pl.pallas_call(
        _combine_kernel,
        grid=(t // tm,),
        in_specs=[spec] * len(args),
        out_specs=pl.BlockSpec((tm, 2 * GROUP_W), lambda i: (i, 0)),
        out_shape=jax.ShapeDtypeStruct((t, 2 * GROUP_W), BF16),
        compiler_params=_params("parallel"),
        name="combine_ab",
    )(*args)


def _attn_c_kernel(sink_ref, q_ref, kp_ref, kc_ref, vp_ref, vc_ref, o_ref, *, slopes):
    i = pl.program_id(1)
    keys = 2 * BLOCK
    dist, valid = _band_masks((i - 1) * BLOCK, C_WINDOW - 1, keys)
    neg_dist = jnp.where(valid, -(1.0 * dist), -jnp.inf)
    low_in = lax.broadcasted_iota(jnp.int32, (keys, LANES), 1) < C_HEAD_DIM
    low_out = lax.broadcasted_iota(jnp.int32, (BLOCK, LANES), 1) < C_HEAD_DIM
    scale = C_HEAD_DIM ** -0.5
    group = C_Q_HEADS // C_KV_HEADS

    def block_diag(x, x_rolled, upper):
        src_low, src_high = (x_rolled, x) if upper else (x, x_rolled)
        zero = jnp.zeros_like(x)
        return jnp.concatenate([jnp.where(low_in, src_low, zero), jnp.where(low_in, zero, src_high)], axis=0)

    for c in range(C_KV_HEADS * C_HEAD_DIM // LANES):
        cs = slice(c * LANES, (c + 1) * LANES)
        kch = jnp.concatenate([kp_ref[:, cs], kc_ref[:, cs]], axis=0)
        vch = jnp.concatenate([vp_ref[:, cs], vc_ref[:, cs]], axis=0)
        kro = pltpu.roll(kch.astype(F32), C_HEAD_DIM, 1).astype(BF16)
        vro = pltpu.roll(vch.astype(F32), C_HEAD_DIM, 1).astype(BF16)
        for e in range(LANES // C_HEAD_DIM):
            kv = c * (LANES // C_HEAD_DIM) + e
            kb = block_diag(kch, kro, e == 1)
            vb = block_diag(vch, vro, e == 1)
            pairs = [kv * group + 2 * p for p in range(group // 2)]
            qcols = [slice(h0 * C_HEAD_DIM, (h0 + 2) * C_HEAD_DIM) for h0 in pairs]
            scores = [lax.dot_general(q_ref[:, qs], kb, (((1,), (1,)), ((), ())), preferred_element_type=F32)
                      for qs in qcols]
            probs, denoms = [], []
            for h0, s in zip(pairs, scores):
                halves, dens = [], []
                for t in range(2):
                    sh = s[:, t * keys:(t + 1) * keys] * scale + slopes[h0 + t] * neg_dist
                    sink = sink_ref[h0 + t]
                    m = jnp.maximum(jnp.max(sh, axis=-1, keepdims=True), sink)
                    pr = jnp.exp(sh - m)
                    dens.append(jnp.sum(pr, axis=-1, keepdims=True) + jnp.exp(sink - m))
                    halves.append(pr.astype(BF16))
                probs.append(jnp.concatenate(halves, axis=1))
                denoms.append(dens)
            for qs, pr, (d0, d1) in zip(qcols, probs, denoms):
                o = jnp.dot(pr, vb, preferred_element_type=F32)
                o_ref[:, qs] = (o / jnp.where(low_out, d0, d1)).astype(o_ref.dtype)


def _attn_c(qkv, sinks, batch, seq):
    ncols = qkv.shape[1]
    qw = C_Q_HEADS * C_HEAD_DIM
    kvw = C_KV_HEADS * C_HEAD_DIM
    view = qkv.reshape(batch, seq, ncols)
    kblk, vblk = qw // kvw, qw // kvw + 1

    def kv_spec(blk, prev):
        def idx(b, i):
            return (b, jnp.maximum(i - 1, 0) if prev else i, blk)
        return pl.BlockSpec((None, BLOCK, kvw), idx)

    out = pl.pallas_call(
        functools.partial(_attn_c_kernel, slopes=_alibi_slopes(C_Q_HEADS)),
        grid=(batch, seq // BLOCK),
        in_specs=[
            pl.BlockSpec(memory_space=pltpu.SMEM),
            pl.BlockSpec((None, BLOCK, qw), lambda b, i: (b, i, 0)),
            kv_spec(kblk, True), kv_spec(kblk, False), kv_spec(vblk, True), kv_spec(vblk, False),
        ],
        out_specs=pl.BlockSpec((None, BLOCK, qw), lambda b, i: (b, i, 0)),
        out_shape=jax.ShapeDtypeStruct((batch, seq, qw), BF16),
        compiler_params=_params("parallel", "arbitrary"),
        name="attn_c",
    )(sinks.astype(F32), view, view, view, view, view)
    return out.reshape(batch * seq, qw)


MM_TN = 512


def _token_tile(k):
    return 1024 if k <= 4096 else 512


def _ffn(x, norm, w_in, w_out, layer):
    h = _rmsnorm(x, norm[layer], BF16)
    hidden = _matmul_swiglu(h, w_in, layer, _token_tile(h.shape[1]), MM_TN)
    return _matmul_residual(hidden, w_out, layer, x, 0.5, _token_tile(hidden.shape[1]), MM_TN)


def _mixer_ab(x, norm, w_in, w_out, layer, batch, seq):
    h = _rmsnorm(x, norm, BF16)
    qkv = _matmul(h, w_in, layer, _token_tile(h.shape[1]), MM_TN)
    groups = [_attn_a_group(qkv, batch, seq, gi) for gi in range(len(A_PAIRS))]
    out_b = _stick_breaking(qkv, batch, seq)
    cat = _combine(groups, out_b)
    return _matmul_residual(cat, w_out, layer, x, 1.0, _token_tile(cat.shape[1]), MM_TN)


def _mixer_c(x, norm, w_in, w_out, sinks, layer, batch, seq):
    h = _rmsnorm(x, norm, BF16)
    qkv = _matmul(h, w_in, layer, _token_tile(h.shape[1]), MM_TN)
    o = _attn_c(qkv, sinks, batch, seq)
    return _matmul_residual(o, w_out, layer, x, 1.0, _token_tile(o.shape[1]), MM_TN)


def kernel(x, ffn1_norm, ffn1_w_in, ffn1_w_out, mix_norm, ab_w_in, ab_w_out, c_w_in, c_w_out, c_sinks,
           ffn2_norm, ffn2_w_in, ffn2_w_out, final_norm):
    batch, seq, d = x.shape
    depth = ffn1_norm.shape[0]
    x = x.reshape(batch * seq, d)
    for l in range(depth):
        x = _ffn(x, ffn1_norm, ffn1_w_in, ffn1_w_out, l)
        if l % 2 == 0:
            x = _mixer_ab(x, mix_norm[l], ab_w_in, ab_w_out, l // 2, batch, seq)
        else:
            x = _mixer_c(x, mix_norm[l], c_w_in, c_w_out, c_sinks[l // 2], l // 2, batch, seq)
        x = _ffn(x, ffn2_norm, ffn2_w_in, ffn2_w_out, l)
    return _rmsnorm(x, final_norm, F32).reshape(batch, seq, d)
```

```python
import functools

import numpy as np
import jax
import jax.numpy as jnp
from jax import lax
from jax.experimental import pallas as pl
from jax.experimental.pallas import tpu as pltpu

F32 = jnp.float32
BF16 = jnp.bfloat16

BLOCK = 128
RMS_EPS = 1e-6
HEAD_DIM_AB = 128
A_PAIRS = ((128, 1), (512, 4), (2048, 16))
GROUP_HEADS = 8
A_HEADS = GROUP_HEADS * len(A_PAIRS)
GROUP_W = GROUP_HEADS * HEAD_DIM_AB
C_HEAD_DIM = 64
C_Q_HEADS = 64
C_KV_HEADS = 8
C_WINDOW = 128
EXP_UNDERFLOW = -106.0

LANES = 128
VMEM_LIMIT_BYTES = 56 * 1024 * 1024


def _alibi_slopes(n):
    return [float(v) for v in np.asarray(2.0 ** (-8.0 * np.arange(1, n + 1) / n), dtype=np.float32)]


def _params(*sem):
    return pltpu.CompilerParams(dimension_semantics=sem, vmem_limit_bytes=VMEM_LIMIT_BYTES)


def _rmsnorm_kernel(x_ref, g_ref, o_ref):
    x = x_ref[...]
    ms = jnp.mean(x * x, axis=-1, keepdims=True)
    o_ref[...] = (x * lax.rsqrt(ms + RMS_EPS) * g_ref[...]).astype(o_ref.dtype)


def _rmsnorm(x, g, out_dtype, tm=256):
    t, d = x.shape
    return pl.pallas_call(
        _rmsnorm_kernel,
        grid=(t // tm,),
        in_specs=[pl.BlockSpec((tm, d), lambda i: (i, 0)), pl.BlockSpec((1, d), lambda i: (0, 0))],
        out_specs=pl.BlockSpec((tm, d), lambda i: (i, 0)),
        out_shape=jax.ShapeDtypeStruct((t, d), out_dtype),
        compiler_params=_params("parallel"),
        name="rmsnorm",
    )(x, g.reshape(1, d).astype(F32))


def _stream_weights(w_hbm, wb_ref, stage_ref, sem_ref, *, layer, col_offsets, tn):
    j, i = pl.program_id(0), pl.program_id(1)
    nj, ni = pl.num_programs(0), pl.num_programs(1)
    rows = wb_ref.shape[2] // ni
    cur = lax.rem(j, 2)

    def chunk_copy(block, chunk, w):
        src = w_hbm.at[layer, pl.ds(pl.multiple_of(chunk * rows, rows), rows),
                       pl.ds(pl.multiple_of((block + col_offsets[w]) * tn, tn), tn)]
        slot = lax.rem(chunk, 2)
        return pltpu.make_async_copy(src, stage_ref.at[slot, w], sem_ref.at[slot, w])

    def start(block, chunk):
        for w in range(len(col_offsets)):
            chunk_copy(block, chunk, w).start()

    def land(block, half, chunk):
        for w in range(len(col_offsets)):
            chunk_copy(block, chunk, w).wait()
            dst = wb_ref.at[half, w]
            dst[pl.ds(pl.multiple_of(chunk * rows, rows), rows), :] = stage_ref[lax.rem(chunk, 2), w].astype(BF16)

    @pl.when((j == 0) & (i == 0))
    def _():
        start(0, 0)

        def fetch(c, carry):
            @pl.when(c + 1 < ni)
            def _():
                start(0, c + 1)
            land(0, 0, c)
            return carry

        lax.fori_loop(0, ni, fetch, 0)

    @pl.when((j > 0) & (i == 0))
    def _():
        land(j, cur, ni - 1)

    @pl.when(j + 1 < nj)
    def _():
        start(j + 1, i)

        @pl.when(i > 0)
        def _():
            land(j + 1, 1 - cur, i - 1)

    return [wb_ref.at[cur, w] for w in range(len(col_offsets))]


def _mm_kernel(a_ref, w_hbm, o_ref, wb_ref, stage_ref, sem_ref, **stream):
    (w_ref,) = _stream_weights(w_hbm, wb_ref, stage_ref, sem_ref, **stream)
    o_ref[...] = jnp.dot(a_ref[...], w_ref[...], preferred_element_type=F32).astype(o_ref.dtype)


def _mm_swiglu_kernel(a_ref, w_hbm, o_ref, wb_ref, stage_ref, sem_ref, **stream):
    wg_ref, wu_ref = _stream_weights(w_hbm, wb_ref, stage_ref, sem_ref, **stream)
    a = a_ref[...]
    g = jnp.dot(a, wg_ref[...], preferred_element_type=F32)
    u = jnp.dot(a, wu_ref[...], preferred_element_type=F32)
    o_ref[...] = (g * jax.nn.sigmoid(g) * u).astype(o_ref.dtype)


def _mm_residual_kernel(a_ref, w_hbm, x_ref, o_ref, wb_ref, stage_ref, sem_ref, *, scale, **stream):
    (w_ref,) = _stream_weights(w_hbm, wb_ref, stage_ref, sem_ref, **stream)
    acc = jnp.dot(a_ref[...], w_ref[...], preferred_element_type=F32)
    o_ref[...] = x_ref[...] + scale * acc


def _weight_scratch(n_w, k, tn, ni):
    assert ni % 2 == 0 and k % ni == 0, "the staging ring alternates slots over an even number of row chunks"
    return [pltpu.VMEM((2, n_w, k, tn), BF16), pltpu.VMEM((2, n_w, k // ni, tn), F32),
            pltpu.SemaphoreType.DMA((2, n_w))]


def _matmul(a, w, layer, tm, tn):
    t, k = a.shape
    n = w.shape[2]
    return pl.pallas_call(
        functools.partial(_mm_kernel, layer=layer, col_offsets=(0,), tn=tn),
        grid=(n // tn, t // tm),
        in_specs=[pl.BlockSpec((tm, k), lambda j, i: (i, 0)), pl.BlockSpec(memory_space=pl.ANY)],
        out_specs=pl.BlockSpec((tm, tn), lambda j, i: (i, j)),
        out_shape=jax.ShapeDtypeStruct((t, n), BF16),
        scratch_shapes=_weight_scratch(1, k, tn, t // tm),
        compiler_params=_params("arbitrary", "arbitrary"),
        name="matmul",
    )(a, w)


def _matmul_swiglu(a, w_in, layer, tm, tn):
    t, k = a.shape
    f = w_in.shape[2] // 2
    nj = f // tn
    return pl.pallas_call(
        functools.partial(_mm_swiglu_kernel, layer=layer, col_offsets=(0, nj), tn=tn),
        grid=(nj, t // tm),
        in_specs=[pl.BlockSpec((tm, k), lambda j, i: (i, 0)), pl.BlockSpec(memory_space=pl.ANY)],
        out_specs=pl.BlockSpec((tm, tn), lambda j, i: (i, j)),
        out_shape=jax.ShapeDtypeStruct((t, f), BF16),
        scratch_shapes=_weight_scratch(2, k, tn, t // tm),
        compiler_params=_params("arbitrary", "arbitrary"),
        name="matmul_swiglu",
    )(a, w_in)


def _matmul_residual(a, w, layer, x, scale, tm, tn):
    t, k = a.shape
    n = w.shape[2]
    return pl.pallas_call(
        functools.partial(_mm_residual_kernel, scale=scale, layer=layer, col_offsets=(0,), tn=tn),
        grid=(n // tn, t // tm),
        in_specs=[
            pl.BlockSpec((tm, k), lambda j, i: (i, 0)),
            pl.BlockSpec(memory_space=pl.ANY),
            pl.BlockSpec((tm, tn), lambda j, i: (i, j)),
        ],
        out_specs=pl.BlockSpec((tm, tn), lambda j, i: (i, j)),
        out_shape=jax.ShapeDtypeStruct((t, n), F32),
        scratch_shapes=_weight_scratch(1, k, tn, t // tm),
        input_output_aliases={2: 0},
        compiler_params=_params("arbitrary", "arbitrary"),
        name="matmul_residual",
    )(a, w, x)


def _band_masks(first_block_offset, max_dist, width):
    qi = lax.broadcasted_iota(jnp.int32, (BLOCK, width), 0)
    si = lax.broadcasted_iota(jnp.int32, (BLOCK, width), 1) & (2 * BLOCK - 1)
    dist = qi - si + BLOCK
    valid = (dist >= 0) & (dist <= max_dist) & (si + first_block_offset >= 0)
    return dist.astype(F32), valid


def _attn_a_kernel(q_ref, kp_ref, kc_ref, vp_ref, vc_ref, o_ref, l_ref, *, slopes, dist_scale, max_dist):
    i = pl.program_id(1)
    dist, valid = _band_masks((i - 1) * BLOCK, max_dist, 2 * BLOCK)
    neg_dist = jnp.where(valid, -(dist_scale * dist), -jnp.inf)
    scale = HEAD_DIM_AB ** -0.5
    cols = [slice(h * HEAD_DIM_AB, (h + 1) * HEAD_DIM_AB) for h in range(GROUP_HEADS)]
    scores = [lax.dot_general(q_ref[:, cs], jnp.concatenate([kp_ref[:, cs], kc_ref[:, cs]], axis=0),
                              (((1,), (1,)), ((), ())), preferred_element_type=F32) for cs in cols]
    probs, stats = [], []
    for h, s in enumerate(scores):
        s = s * scale + slopes[h] * neg_dist
        m = jnp.max(s, axis=-1, keepdims=True)
        p = jnp.exp(s - m)
        stats.append((m, jnp.sum(p, axis=-1, keepdims=True)))
        probs.append(p.astype(BF16))
    for cs, p, (m, denom) in zip(cols, probs, stats):
        v = jnp.concatenate([vp_ref[:, cs], vc_ref[:, cs]], axis=0)
        o_ref[:, cs] = jnp.dot(p, v, preferred_element_type=F32) / denom
        l_ref[:, cs] = jnp.broadcast_to(m + jnp.log(denom), (BLOCK, HEAD_DIM_AB))


def _attn_a_group(qkv, batch, seq, gi):
    window, r = A_PAIRS[gi]
    a_blocks = A_HEADS * HEAD_DIM_AB // GROUP_W
    lr = seq // r
    slopes = _alibi_slopes(A_HEADS)[gi * GROUP_HEADS:(gi + 1) * GROUP_HEADS]
    if r == 1:
        src = qkv.reshape(batch, seq, qkv.shape[1])
        col_blocks = [section * a_blocks + gi for section in range(3)]
    else:
        cols = [qkv[:, (section * a_blocks + gi) * GROUP_W:(section * a_blocks + gi + 1) * GROUP_W] for section in range(3)]
        src = jnp.concatenate(cols, axis=1).reshape(batch, lr, r, 3 * GROUP_W)
        src = src.swapaxes(1, 2).reshape(batch * r, lr, 3 * GROUP_W)
        col_blocks = [0, 1, 2]

    def spec(section, prev):
        def idx(n, i):
            return (n, jnp.maximum(i - 1, 0) if prev else i, col_blocks[section])
        return pl.BlockSpec((None, BLOCK, GROUP_W), idx)

    out_spec = pl.BlockSpec((None, BLOCK, GROUP_W), lambda n, i: (n, i, 0))
    shape = jax.ShapeDtypeStruct((batch * r, lr, GROUP_W), F32)
    o, l = pl.pallas_call(
        functools.partial(_attn_a_kernel, slopes=slopes, dist_scale=float(r), max_dist=window // r),
        grid=(batch * r, lr // BLOCK),
        in_specs=[spec(0, False), spec(1, True), spec(1, False), spec(2, True), spec(2, False)],
        out_specs=[out_spec, out_spec],
        out_shape=[shape, shape],
        compiler_params=_params("parallel", "arbitrary"),
        name=f"attn_a_r{r}",
    )(src, src, src, src, src)

    def undilate(t):
        return t.reshape(batch, r, lr, GROUP_W).swapaxes(1, 2).reshape(batch * seq, GROUP_W)

    return undilate(o), undilate(l)


def _sb_kernel(q_ref, k_ref, v_ref, o_ref, carry_ref, acc_ref, *, tile):
    iq = pl.program_id(1)
    scale = HEAD_DIM_AB ** -0.5
    row = lax.broadcasted_iota(jnp.int32, (tile, tile), 0)
    col = lax.broadcasted_iota(jnp.int32, (tile, tile), 1)
    suffix = jnp.concatenate([(row > col).astype(BF16), jnp.ones((tile, tile), BF16)], axis=1)
    suffix = jnp.concatenate([suffix, suffix], axis=0)
    causal = col < row
    heads = range(GROUP_HEADS)
    cols = [slice(h * HEAD_DIM_AB, (h + 1) * HEAD_DIM_AB) for h in heads]

    def step(j, diagonal):
        start = pl.multiple_of(j * tile, tile)
        zs = [lax.dot_general(q_ref[:, cs], k_ref[pl.ds(start, tile), cs], (((1,), (1,)), ((), ())),
                              preferred_element_type=F32) * scale for cs in cols]
        log_hits, splits = [], []
        for z in zs:
            tail = jnp.log(1.0 + jnp.exp(-jnp.abs(z)))
            log_keep = -(jnp.maximum(z, 0.0) + tail)
            log_hits.append(jnp.minimum(z, 0.0) - tail)
            if diagonal:
                log_keep = jnp.where(causal, log_keep, 0.0)
            hi = log_keep.astype(BF16)
            lo = (log_keep - hi.astype(F32)).astype(BF16)
            splits.append(jnp.concatenate([hi, lo], axis=1))
        sums = jnp.dot(jnp.concatenate(splits, axis=0), suffix, preferred_element_type=F32)
        for h, cs in zip(heads, cols):
            after = sums[h * tile:(h + 1) * tile, :tile]
            rowsum = sums[h * tile:(h + 1) * tile, tile:]
            if not diagonal:
                after = after + carry_ref[:, cs]
            a = jnp.exp(log_hits[h] + after)
            if diagonal:
                a = jnp.where(causal, a, 0.0)
            pv = jnp.dot(a.astype(BF16), v_ref[pl.ds(start, tile), cs], preferred_element_type=F32)
            if diagonal:
                acc_ref[:, cs] = pv
                carry_ref[:, cs] = rowsum
            else:
                acc_ref[:, cs] += pv
                carry_ref[:, cs] += rowsum

    def any_row_alive():
        c = carry_ref[:, cols[0]]
        for cs in cols[1:]:
            c = jnp.maximum(c, carry_ref[:, cs])
        return jnp.max(c) >= EXP_UNDERFLOW

    step(iq, True)

    def cond(state):
        return jnp.logical_and(state[0] <= iq, state[1])

    def body(state):
        step(iq - state[0], False)
        return state[0] + 1, any_row_alive()

    lax.while_loop(cond, body, (jnp.int32(1), any_row_alive()))
    o_ref[...] = acc_ref[...].astype(o_ref.dtype)


def _stick_breaking(qkv, batch, seq, tile=BLOCK):
    ncols = qkv.shape[1]
    view = qkv.reshape(batch, seq, ncols)
    base = 3 * A_HEADS * HEAD_DIM_AB // GROUP_W
    resident = dict(pipeline_mode=pl.Buffered(1))
    out = pl.pallas_call(
        functools.partial(_sb_kernel, tile=tile),
        grid=(batch, seq // tile),
        in_specs=[
            pl.BlockSpec((None, tile, GROUP_W), lambda b, i: (b, i, base)),
            pl.BlockSpec((None, seq, GROUP_W), lambda b, i: (b, 0, base + 1), **resident),
            pl.BlockSpec((None, seq, GROUP_W), lambda b, i: (b, 0, base + 2), **resident),
        ],
        out_specs=pl.BlockSpec((None, tile, GROUP_W), lambda b, i: (b, i, 0)),
        out_shape=jax.ShapeDtypeStruct((batch, seq, GROUP_W), BF16),
        scratch_shapes=[pltpu.VMEM((tile, GROUP_W), F32), pltpu.VMEM((tile, GROUP_W), F32)],
        compiler_params=_params("parallel", "arbitrary"),
        name="stick_breaking",
    )(view, view, view)
    return out.reshape(batch * seq, GROUP_W)


def _combine_kernel(o1_ref, l1_ref, o2_ref, l2_ref, o3_ref, l3_ref, ob_ref, out_ref):
    l1, l2, l3 = l1_ref[...], l2_ref[...], l3_ref[...]
    m = jnp.maximum(l1, jnp.maximum(l2, l3))
    e1, e2, e3 = jnp.exp(l1 - m), jnp.exp(l2 - m), jnp.exp(l3 - m)
    num = e1 * o1_ref[...] + e2 * o2_ref[...] + e3 * o3_ref[...]
    out_ref[:, :GROUP_W] = (num / (e1 + e2 + e3)).astype(out_ref.dtype)
    out_ref[:, GROUP_W:] = ob_ref[...]


def _combine(groups, out_b, tm=512):
    t = out_b.shape[0]
    spec = pl.BlockSpec((tm, GROUP_W), lambda i: (i, 0))
    args = [a for pair in groups for a in pair] + [out_b]
    return pl.pallas_call(
        _combine_kernel,
        grid=(t // tm,),
        in_specs=[spec] * len(args),
        out_specs=pl.BlockSpec((tm, 2 * GROUP_W), lambda i: (i, 0)),
        out_shape=jax.ShapeDtypeStruct((t, 2 * GROUP_W), BF16),
        compiler_params=_params("parallel"),
        name="combine_ab",
    )(*args)


def _attn_c_kernel(sink_ref, q_ref, kp_ref, kc_ref, vp_ref, vc_ref, o_ref, *, slopes):
    i = pl.program_id(1)
    keys = 2 * BLOCK
    dist, valid = _band_masks((i - 1) * BLOCK, C_WINDOW - 1, keys)
    neg_dist = jnp.where(valid, -(1.0 * dist), -jnp.inf)
    low_in = lax.broadcasted_iota(jnp.int32, (keys, LANES), 1) < C_HEAD_DIM
    low_out = lax.broadcasted_iota(jnp.int32, (BLOCK, LANES), 1) < C_HEAD_DIM
    scale = C_HEAD_DIM ** -0.5
    group = C_Q_HEADS // C_KV_HEADS

    def block_diag(x, x_rolled, upper):
        src_low, src_high = (x_rolled, x) if upper else (x, x_rolled)
        zero = jnp.zeros_like(x)
        return jnp.concatenate([jnp.where(low_in, src_low, zero), jnp.where(low_in, zero, src_high)], axis=0)

    for c in range(C_KV_HEADS * C_HEAD_DIM // LANES):
        cs = slice(c * LANES, (c + 1) * LANES)
        kch = jnp.concatenate([kp_ref[:, cs], kc_ref[:, cs]], axis=0)
        vch = jnp.concatenate([vp_ref[:, cs], vc_ref[:, cs]], axis=0)
        kro = pltpu.roll(kch.astype(F32), C_HEAD_DIM, 1).astype(BF16)
        vro = pltpu.roll(vch.astype(F32), C_HEAD_DIM, 1).astype(BF16)
        for e in range(LANES // C_HEAD_DIM):
            kv = c * (LANES // C_HEAD_DIM) + e
            kb = block_diag(kch, kro, e == 1)
            vb = block_diag(vch, vro, e == 1)
            pairs = [kv * group + 2 * p for p in range(group // 2)]
            qcols = [slice(h0 * C_HEAD_DIM, (h0 + 2) * C_HEAD_DIM) for h0 in pairs]
            scores = [lax.dot_general(q_ref[:, qs], kb, (((1,), (1,)), ((), ())), preferred_element_type=F32)
                      for qs in qcols]
            probs, denoms = [], []
            for h0, s in zip(pairs, scores):
                halves, dens = [], []
                for t in range(2):
                    sh = s[:, t * keys:(t + 1) * keys] * scale + slopes[h0 + t] * neg_dist
                    sink = sink_ref[h0 + t]
                    m = jnp.maximum(jnp.max(sh, axis=-1, keepdims=True), sink)
                    pr = jnp.exp(sh - m)
                    dens.append(jnp.sum(pr, axis=-1, keepdims=True) + jnp.exp(sink - m))
                    halves.append(pr.astype(BF16))
                probs.append(jnp.concatenate(halves, axis=1))
                denoms.append(dens)
            for qs, pr, (d0, d1) in zip(qcols, probs, denoms):
                o = jnp.dot(pr, vb, preferred_element_type=F32)
                o_ref[:, qs] = (o / jnp.where(low_out, d0, d1)).astype(o_ref.dtype)


def _attn_c(qkv, sinks, batch, seq):
    ncols = qkv.shape[1]
    qw = C_Q_HEADS * C_HEAD_DIM
    kvw = C_KV_HEADS * C_HEAD_DIM
    view = qkv.reshape(batch, seq, ncols)
    kblk, vblk = qw // kvw, qw // kvw + 1

    def kv_spec(blk, prev):
        def idx(b, i):
            return (b, jnp.maximum(i - 1, 0) if prev else i, blk)
        return pl.BlockSpec((None, BLOCK, kvw), idx)

    out = pl.pallas_call(
        functools.partial(_attn_c_kernel, slopes=_alibi_slopes(C_Q_HEADS)),
        grid=(batch, seq // BLOCK),
        in_specs=[
            pl.BlockSpec(memory_space=pltpu.SMEM),
            pl.BlockSpec((None, BLOCK, qw), lambda b, i: (b, i, 0)),
            kv_spec(kblk, True), kv_spec(kblk, False), kv_spec(vblk, True), kv_spec(vblk, False),
        ],
        out_specs=pl.BlockSpec((None, BLOCK, qw), lambda b, i: (b, i, 0)),
        out_shape=jax.ShapeDtypeStruct((batch, seq, qw), BF16),
        compiler_params=_params("parallel", "arbitrary"),
        name="attn_c",
    )(sinks.astype(F32), view, view, view, view, view)
    return out.reshape(batch * seq, qw)


MM_TM = 1024
MM_TN = 512


def _ffn(x, norm, w_in, w_out, layer):
    h = _rmsnorm(x, norm[layer], BF16)
    hidden = _matmul_swiglu(h, w_in, layer, MM_TM, MM_TN)
    return _matmul_residual(hidden, w_out, layer, x, 0.5, MM_TM, MM_TN)


def _mixer_ab(x, norm, w_in, w_out, layer, batch, seq):
    h = _rmsnorm(x, norm, BF16)
    qkv = _matmul(h, w_in, layer, MM_TM, MM_TN)
    groups = [_attn_a_group(qkv, batch, seq, gi) for gi in range(len(A_PAIRS))]
    out_b = _stick_breaking(qkv, batch, seq)
    cat = _combine(groups, out_b)
    return _matmul_residual(cat, w_out, layer, x, 1.0, MM_TM, MM_TN)


def _mixer_c(x, norm, w_in, w_out, sinks, layer, batch, seq):
    h = _rmsnorm(x, norm, BF16)
    qkv = _matmul(h, w_in, layer, MM_TM, MM_TN)
    o = _attn_c(qkv, sinks, batch, seq)
    return _matmul_residual(o, w_out, layer, x, 1.0, MM_TM, MM_TN)


def kernel(x, ffn1_norm, ffn1_w_in, ffn1_w_out, mix_norm, ab_w_in, ab_w_out, c_w_in, c_w_out, c_sinks,
           ffn2_norm, ffn2_w_in, ffn2_w_out, final_norm):
    batch, seq, d = x.shape
    depth = ffn1_norm.shape[0]
    x = x.reshape(batch * seq, d)
    for l in range(depth):
        x = _ffn(x, ffn1_norm, ffn1_w_in, ffn1_w_out, l)
        if l % 2 == 0:
            x = _mixer_ab(x, mix_norm[l], ab_w_in, ab_w_out, l // 2, batch, seq)
        else:
            x = _mixer_c(x, mix_norm[l], c_w_in, c_w_out, c_sinks[l // 2], l // 2, batch, seq)
        x = _ffn(x, ffn2_norm, ffn2_w_in, ffn2_w_out, l)
    return _rmsnorm(x, final_norm, F32).reshape(batch, seq, d)
```

```python
import functools

import numpy as np
import jax
import jax.numpy as jnp
from jax import lax
from jax.experimental import pallas as pl
from jax.experimental.pallas import tpu as pltpu

F32 = jnp.float32
BF16 = jnp.bfloat16

BLOCK = 128
RMS_EPS = 1e-6
HEAD_DIM_AB = 128
A_PAIRS = ((128, 1), (512, 4), (2048, 16))
GROUP_HEADS = 8
A_HEADS = GROUP_HEADS * len(A_PAIRS)
GROUP_W = GROUP_HEADS * HEAD_DIM_AB
C_HEAD_DIM = 64
C_Q_HEADS = 64
C_KV_HEADS = 8
C_WINDOW = 128
EXP_UNDERFLOW = -106.0

LANES = 128
LSE_LANES = LANES // GROUP_HEADS
VMEM_LIMIT_BYTES = 56 * 1024 * 1024


def _alibi_slopes(n):
    return [float(v) for v in np.asarray(2.0 ** (-8.0 * np.arange(1, n + 1) / n), dtype=np.float32)]


def _params(*sem):
    return pltpu.CompilerParams(dimension_semantics=sem, vmem_limit_bytes=VMEM_LIMIT_BYTES)


def _rmsnorm_kernel(x_ref, g_ref, o_ref):
    x = x_ref[...]
    ms = jnp.mean(x * x, axis=-1, keepdims=True)
    o_ref[...] = (x * lax.rsqrt(ms + RMS_EPS) * g_ref[...]).astype(o_ref.dtype)


def _rmsnorm(x, g, out_dtype, tm=256):
    t, d = x.shape
    return pl.pallas_call(
        _rmsnorm_kernel,
        grid=(t // tm,),
        in_specs=[pl.BlockSpec((tm, d), lambda i: (i, 0)), pl.BlockSpec((1, d), lambda i: (0, 0))],
        out_specs=pl.BlockSpec((tm, d), lambda i: (i, 0)),
        out_shape=jax.ShapeDtypeStruct((t, d), out_dtype),
        compiler_params=_params("parallel"),
        name="rmsnorm",
    )(x, g.reshape(1, d).astype(F32))


def _stream_weights(w_hbm, wb_ref, stage_ref, sem_ref, *, layer, col_offsets, tn):
    j, i = pl.program_id(0), pl.program_id(1)
    nj, ni = pl.num_programs(0), pl.num_programs(1)
    rows = wb_ref.shape[2] // ni
    cur = lax.rem(j, 2)

    def chunk_copy(block, chunk, w):
        src = w_hbm.at[layer, pl.ds(pl.multiple_of(chunk * rows, rows), rows),
                       pl.ds(pl.multiple_of((block + col_offsets[w]) * tn, tn), tn)]
        slot = lax.rem(chunk, 2)
        return pltpu.make_async_copy(src, stage_ref.at[slot, w], sem_ref.at[slot, w])

    def start(block, chunk):
        for w in range(len(col_offsets)):
            chunk_copy(block, chunk, w).start()

    def land(block, half, chunk):
        for w in range(len(col_offsets)):
            chunk_copy(block, chunk, w).wait()
            dst = wb_ref.at[half, w]
            dst[pl.ds(pl.multiple_of(chunk * rows, rows), rows), :] = stage_ref[lax.rem(chunk, 2), w].astype(BF16)

    @pl.when((j == 0) & (i == 0))
    def _():
        start(0, 0)

        def fetch(c, carry):
            @pl.when(c + 1 < ni)
            def _():
                start(0, c + 1)
            land(0, 0, c)
            return carry

        lax.fori_loop(0, ni, fetch, 0)

    @pl.when((j > 0) & (i == 0))
    def _():
        land(j, cur, ni - 1)

    @pl.when(j + 1 < nj)
    def _():
        start(j + 1, i)

        @pl.when(i > 0)
        def _():
            land(j + 1, 1 - cur, i - 1)

    return [wb_ref.at[cur, w] for w in range(len(col_offsets))]


def _mm_kernel(a_ref, w_hbm, o_ref, wb_ref, stage_ref, sem_ref, **stream):
    (w_ref,) = _stream_weights(w_hbm, wb_ref, stage_ref, sem_ref, **stream)
    o_ref[...] = jnp.dot(a_ref[...], w_ref[...], preferred_element_type=F32).astype(o_ref.dtype)


def _mm_swiglu_kernel(a_ref, w_hbm, o_ref, wb_ref, stage_ref, sem_ref, **stream):
    wg_ref, wu_ref = _stream_weights(w_hbm, wb_ref, stage_ref, sem_ref, **stream)
    a = a_ref[...]
    g = jnp.dot(a, wg_ref[...], preferred_element_type=F32)
    u = jnp.dot(a, wu_ref[...], preferred_element_type=F32)
    o_ref[...] = (g * jax.nn.sigmoid(g) * u).astype(o_ref.dtype)


def _mm_residual_kernel(a_ref, w_hbm, x_ref, o_ref, wb_ref, stage_ref, sem_ref, *, scale, **stream):
    (w_ref,) = _stream_weights(w_hbm, wb_ref, stage_ref, sem_ref, **stream)
    acc = jnp.dot(a_ref[...], w_ref[...], preferred_element_type=F32)
    o_ref[...] = x_ref[...] + scale * acc


def _weight_scratch(n_w, k, tn, ni):
    assert ni % 2 == 0 and k % ni == 0, "the staging ring alternates slots over an even number of row chunks"
    return [pltpu.VMEM((2, n_w, k, tn), BF16), pltpu.VMEM((2, n_w, k // ni, tn), F32),
            pltpu.SemaphoreType.DMA((2, n_w))]


def _matmul(a, w, layer, tm, tn):
    t, k = a.shape
    n = w.shape[2]
    return pl.pallas_call(
        functools.partial(_mm_kernel, layer=layer, col_offsets=(0,), tn=tn),
        grid=(n // tn, t // tm),
        in_specs=[pl.BlockSpec((tm, k), lambda j, i: (i, 0)), pl.BlockSpec(memory_space=pl.ANY)],
        out_specs=pl.BlockSpec((tm, tn), lambda j, i: (i, j)),
        out_shape=jax.ShapeDtypeStruct((t, n), BF16),
        scratch_shapes=_weight_scratch(1, k, tn, t // tm),
        compiler_params=_params("arbitrary", "arbitrary"),
        name="matmul",
    )(a, w)


def _matmul_swiglu(a, w_in, layer, tm, tn):
    t, k = a.shape
    f = w_in.shape[2] // 2
    nj = f // tn
    return pl.pallas_call(
        functools.partial(_mm_swiglu_kernel, layer=layer, col_offsets=(0, nj), tn=tn),
        grid=(nj, t // tm),
        in_specs=[pl.BlockSpec((tm, k), lambda j, i: (i, 0)), pl.BlockSpec(memory_space=pl.ANY)],
        out_specs=pl.BlockSpec((tm, tn), lambda j, i: (i, j)),
        out_shape=jax.ShapeDtypeStruct((t, f), BF16),
        scratch_shapes=_weight_scratch(2, k, tn, t // tm),
        compiler_params=_params("arbitrary", "arbitrary"),
        name="matmul_swiglu",
    )(a, w_in)


def _matmul_residual(a, w, layer, x, scale, tm, tn, in_place=True):
    t, k = a.shape
    n = w.shape[2]
    return pl.pallas_call(
        functools.partial(_mm_residual_kernel, scale=scale, layer=layer, col_offsets=(0,), tn=tn),
        grid=(n // tn, t // tm),
        in_specs=[
            pl.BlockSpec((tm, k), lambda j, i: (i, 0)),
            pl.BlockSpec(memory_space=pl.ANY),
            pl.BlockSpec((tm, tn), lambda j, i: (i, j)),
        ],
        out_specs=pl.BlockSpec((tm, tn), lambda j, i: (i, j)),
        out_shape=jax.ShapeDtypeStruct((t, n), F32),
        scratch_shapes=_weight_scratch(1, k, tn, t // tm),
        input_output_aliases={2: 0} if in_place else {},
        compiler_params=_params("arbitrary", "arbitrary"),
        name="matmul_residual",
    )(a, w, x)


def _band_masks(first_block_offset, max_dist, width):
    qi = lax.broadcasted_iota(jnp.int32, (BLOCK, width), 0)
    si = lax.broadcasted_iota(jnp.int32, (BLOCK, width), 1) & (2 * BLOCK - 1)
    dist = qi - si + BLOCK
    valid = (dist >= 0) & (dist <= max_dist) & (si + first_block_offset >= 0)
    return dist.astype(F32), valid


def _attn_a_kernel(q_ref, kp_ref, kc_ref, vp_ref, vc_ref, o_ref, l_ref, *, slopes, dist_scale, max_dist):
    i = pl.program_id(1)
    dist, valid = _band_masks((i - 1) * BLOCK, max_dist, 2 * BLOCK)
    neg_dist = jnp.where(valid, -(dist_scale * dist), -jnp.inf)
    scale = HEAD_DIM_AB ** -0.5
    cols = [slice(h * HEAD_DIM_AB, (h + 1) * HEAD_DIM_AB) for h in range(GROUP_HEADS)]
    scores = [lax.dot_general(q_ref[:, cs], jnp.concatenate([kp_ref[:, cs], kc_ref[:, cs]], axis=0),
                              (((1,), (1,)), ((), ())), preferred_element_type=F32) for cs in cols]
    probs, stats = [], []
    for h, s in enumerate(scores):
        s = s * scale + slopes[h] * neg_dist
        m = jnp.max(s, axis=-1, keepdims=True)
        p = jnp.exp(s - m)
        stats.append((m, jnp.sum(p, axis=-1, keepdims=True)))
        probs.append(p.astype(BF16))
    lane_head = lax.broadcasted_iota(jnp.int32, (BLOCK, LANES), 1) // LSE_LANES
    lse = jnp.zeros((BLOCK, LANES), F32)
    for h, (cs, p, (m, denom)) in enumerate(zip(cols, probs, stats)):
        v = jnp.concatenate([vp_ref[:, cs], vc_ref[:, cs]], axis=0)
        o_ref[:, cs] = (jnp.dot(p, v, preferred_element_type=F32) / denom).astype(o_ref.dtype)
        lse = jnp.where(lane_head == h, m + jnp.log(denom), lse)
    l_ref[...] = lse


def _attn_a_group(qkv, batch, seq, gi):
    window, r = A_PAIRS[gi]
    a_blocks = A_HEADS * HEAD_DIM_AB // GROUP_W
    lr = seq // r
    slopes = _alibi_slopes(A_HEADS)[gi * GROUP_HEADS:(gi + 1) * GROUP_HEADS]
    if r == 1:
        src = qkv.reshape(batch, seq, qkv.shape[1])
        col_blocks = [section * a_blocks + gi for section in range(3)]
    else:
        cols = [qkv[:, (section * a_blocks + gi) * GROUP_W:(section * a_blocks + gi + 1) * GROUP_W] for section in range(3)]
        src = jnp.concatenate(cols, axis=1).reshape(batch, lr, r, 3 * GROUP_W)
        src = src.swapaxes(1, 2).reshape(batch * r, lr, 3 * GROUP_W)
        col_blocks = [0, 1, 2]

    def spec(section, prev):
        def idx(n, i):
            return (n, jnp.maximum(i - 1, 0) if prev else i, col_blocks[section])
        return pl.BlockSpec((None, BLOCK, GROUP_W), idx)

    o, l = pl.pallas_call(
        functools.partial(_attn_a_kernel, slopes=slopes, dist_scale=float(r), max_dist=window // r),
        grid=(batch * r, lr // BLOCK),
        in_specs=[spec(0, False), spec(1, True), spec(1, False), spec(2, True), spec(2, False)],
        out_specs=[pl.BlockSpec((None, BLOCK, GROUP_W), lambda n, i: (n, i, 0)),
                   pl.BlockSpec((None, BLOCK, LANES), lambda n, i: (n, i, 0))],
        out_shape=[jax.ShapeDtypeStruct((batch * r, lr, GROUP_W), BF16),
                   jax.ShapeDtypeStruct((batch * r, lr, LANES), F32)],
        compiler_params=_params("parallel", "arbitrary"),
        name=f"attn_a_r{r}",
    )(src, src, src, src, src)

    def undilate(t):
        return t.reshape(batch, r, lr, t.shape[-1]).swapaxes(1, 2).reshape(batch * seq, t.shape[-1])

    return undilate(o), undilate(l)


def _sb_kernel(q_ref, k_ref, v_ref, o_ref, carry_ref, acc_ref, *, tile):
    iq = pl.program_id(1)
    scale = HEAD_DIM_AB ** -0.5
    row = lax.broadcasted_iota(jnp.int32, (tile, tile), 0)
    col = lax.broadcasted_iota(jnp.int32, (tile, tile), 1)
    suffix = jnp.concatenate([(row > col).astype(BF16), jnp.ones((tile, tile), BF16)], axis=1)
    suffix = jnp.concatenate([suffix, suffix], axis=0)
    causal = col < row
    heads = range(GROUP_HEADS)
    cols = [slice(h * HEAD_DIM_AB, (h + 1) * HEAD_DIM_AB) for h in heads]

    def step(j, diagonal):
        start = pl.multiple_of(j * tile, tile)
        zs = [lax.dot_general(q_ref[:, cs], k_ref[pl.ds(start, tile), cs], (((1,), (1,)), ((), ())),
                              preferred_element_type=F32) * scale for cs in cols]
        log_hits, splits = [], []
        for z in zs:
            tail = jnp.log(1.0 + jnp.exp(-jnp.abs(z)))
            log_keep = -(jnp.maximum(z, 0.0) + tail)
            log_hits.append(jnp.minimum(z, 0.0) - tail)
            if diagonal:
                log_keep = jnp.where(causal, log_keep, 0.0)
            hi = log_keep.astype(BF16)
            lo = (log_keep - hi.astype(F32)).astype(BF16)
            splits.append(jnp.concatenate([hi, lo], axis=1))
        sums = jnp.dot(jnp.concatenate(splits, axis=0), suffix, preferred_element_type=F32)
        for h, cs in zip(heads, cols):
            after = sums[h * tile:(h + 1) * tile, :tile]
            rowsum = sums[h * tile:(h + 1) * tile, tile:]
            if not diagonal:
                after = after + carry_ref[:, cs]
            a = jnp.exp(log_hits[h] + after)
            if diagonal:
                a = jnp.where(causal, a, 0.0)
            pv = jnp.dot(a.astype(BF16), v_ref[pl.ds(start, tile), cs], preferred_element_type=F32)
            if diagonal:
                acc_ref[:, cs] = pv
                carry_ref[:, cs] = rowsum
            else:
                acc_ref[:, cs] += pv
                carry_ref[:, cs] += rowsum

    def any_row_alive():
        c = carry_ref[:, cols[0]]
        for cs in cols[1:]:
            c = jnp.maximum(c, carry_ref[:, cs])
        return jnp.max(c) >= EXP_UNDERFLOW

    step(iq, True)

    def cond(state):
        return jnp.logical_and(state[0] <= iq, state[1])

    def body(state):
        step(iq - state[0], False)
        return state[0] + 1, any_row_alive()

    lax.while_loop(cond, body, (jnp.int32(1), any_row_alive()))
    o_ref[...] = acc_ref[...].astype(o_ref.dtype)


def _stick_breaking(qkv, batch, seq, tile=BLOCK):
    ncols = qkv.shape[1]
    view = qkv.reshape(batch, seq, ncols)
    base = 3 * A_HEADS * HEAD_DIM_AB // GROUP_W
    resident = dict(pipeline_mode=pl.Buffered(1))
    out = pl.pallas_call(
        functools.partial(_sb_kernel, tile=tile),
        grid=(batch, seq // tile),
        in_specs=[
            pl.BlockSpec((None, tile, GROUP_W), lambda b, i: (b, i, base)),
            pl.BlockSpec((None, seq, GROUP_W), lambda b, i: (b, 0, base + 1), **resident),
            pl.BlockSpec((None, seq, GROUP_W), lambda b, i: (b, 0, base + 2), **resident),
        ],
        out_specs=pl.BlockSpec((None, tile, GROUP_W), lambda b, i: (b, i, 0)),
        out_shape=jax.ShapeDtypeStruct((batch, seq, GROUP_W), BF16),
        scratch_shapes=[pltpu.VMEM((tile, GROUP_W), F32), pltpu.VMEM((tile, GROUP_W), F32)],
        compiler_params=_params("parallel", "arbitrary"),
        name="stick_breaking",
    )(view, view, view)
    return out.reshape(batch * seq, GROUP_W)


def _combine_kernel(o1_ref, l1_ref, o2_ref, l2_ref, o3_ref, l3_ref, ob_ref, out_ref):
    l1, l2, l3 = l1_ref[...], l2_ref[...], l3_ref[...]
    m = jnp.maximum(l1, jnp.maximum(l2, l3))
    e1, e2, e3 = jnp.exp(l1 - m), jnp.exp(l2 - m), jnp.exp(l3 - m)
    den = e1 + e2 + e3
    lane = lax.broadcasted_iota(jnp.int32, (2 * LANES, GROUP_W), 0) % LANES
    head = lax.broadcasted_iota(jnp.int32, (2 * LANES, GROUP_W), 1) // HEAD_DIM_AB
    spread = (lane == head * LSE_LANES).astype(BF16)
    num = jnp.zeros(o1_ref.shape, F32)
    for e, o_ref in ((e1, o1_ref), (e2, o2_ref), (e3, o3_ref)):
        w = e / den
        hi = w.astype(BF16)
        lo = (w - hi.astype(F32)).astype(BF16)
        wide = jnp.dot(jnp.concatenate([hi, lo], axis=1), spread, preferred_element_type=F32)
        num = num + wide * o_ref[...].astype(F32)
    out_ref[:, :GROUP_W] = num.astype(out_ref.dtype)
    out_ref[:, GROUP_W:] = ob_ref[...]


def _combine(groups, out_b, tm=512):
    t = out_b.shape[0]
    spec = pl.BlockSpec((tm, GROUP_W), lambda i: (i, 0))
    lse_spec = pl.BlockSpec((tm, LANES), lambda i: (i, 0))
    args = [a for pair in groups for a in pair] + [out_b]
    return pl.pallas_call(
        _combine_kernel,
        grid=(t // tm,),
        in_specs=[spec, lse_spec] * len(groups) + [spec],
        out_specs=pl.BlockSpec((tm, 2 * GROUP_W), lambda i: (i, 0)),
        out_shape=jax.ShapeDtypeStruct((t, 2 * GROUP_W), BF16),
        compiler_params=_params("parallel"),
        name="combine_ab",
    )(*args)


def _attn_c_kernel(sink_ref, q_ref, kp_ref, kc_ref, vp_ref, vc_ref, o_ref, *, slopes):
    i = pl.program_id(1)
    keys = 2 * BLOCK
    dist, valid = _band_masks((i - 1) * BLOCK, C_WINDOW - 1, keys)
    neg_dist = jnp.where(valid, -(1.0 * dist), -jnp.inf)
    low_in = lax.broadcasted_iota(jnp.int32, (keys, LANES), 1) < C_HEAD_DIM
    low_out = lax.broadcasted_iota(jnp.int32, (BLOCK, LANES), 1) < C_HEAD_DIM
    scale = C_HEAD_DIM ** -0.5
    group = C_Q_HEADS // C_KV_HEADS

    def block_diag(x, x_rolled, upper):
        src_low, src_high = (x_rolled, x) if upper else (x, x_rolled)
        zero = jnp.zeros_like(x)
        return jnp.concatenate([jnp.where(low_in, src_low, zero), jnp.where(low_in, zero, src_high)], axis=0)

    for c in range(C_KV_HEADS * C_HEAD_DIM // LANES):
        cs = slice(c * LANES, (c + 1) * LANES)
        kch = jnp.concatenate([kp_ref[:, cs], kc_ref[:, cs]], axis=0)
        vch = jnp.concatenate([vp_ref[:, cs], vc_ref[:, cs]], axis=0)
        kro = pltpu.roll(kch.astype(F32), C_HEAD_DIM, 1).astype(BF16)
        vro = pltpu.roll(vch.astype(F32), C_HEAD_DIM, 1).astype(BF16)
        for e in range(LANES // C_HEAD_DIM):
            kv = c * (LANES // C_HEAD_DIM) + e
            kb = block_diag(kch, kro, e == 1)
            vb = block_diag(vch, vro, e == 1)
            pairs = [kv * group + 2 * p for p in range(group // 2)]
            qcols = [slice(h0 * C_HEAD_DIM, (h0 + 2) * C_HEAD_DIM) for h0 in pairs]
            scores = [lax.dot_general(q_ref[:, qs], kb, (((1,), (1,)), ((), ())), preferred_element_type=F32)
                      for qs in qcols]
            probs, denoms = [], []
            for h0, s in zip(pairs, scores):
                halves, dens = [], []
                for t in range(2):
                    sh = s[:, t * keys:(t + 1) * keys] * scale + slopes[h0 + t] * neg_dist
                    sink = sink_ref[h0 + t]
                    m = jnp.maximum(jnp.max(sh, axis=-1, keepdims=True), sink)
                    pr = jnp.exp(sh - m)
                    dens.append(jnp.sum(pr, axis=-1, keepdims=True) + jnp.exp(sink - m))
                    halves.append(pr.astype(BF16))
                probs.append(jnp.concatenate(halves, axis=1))
                denoms.append(dens)
            for qs, pr, (d0, d1) in zip(qcols, probs, denoms):
                o = jnp.dot(pr, vb, preferred_element_type=F32)
                o_ref[:, qs] = (o / jnp.where(low_out, d0, d1)).astype(o_ref.dtype)


def _attn_c(qkv, sinks, batch, seq):
    ncols = qkv.shape[1]
    qw = C_Q_HEADS * C_HEAD_DIM
    kvw = C_KV_HEADS * C_HEAD_DIM
    view = qkv.reshape(batch, seq, ncols)
    kblk, vblk = qw // kvw, qw // kvw + 1

    def kv_spec(blk, prev):
        def idx(b, i):
            return (b, jnp.maximum(i - 1, 0) if prev else i, blk)
        return pl.BlockSpec((None, BLOCK, kvw), idx)

    out = pl.pallas_call(
        functools.partial(_attn_c_kernel, slopes=_alibi_slopes(C_Q_HEADS)),
        grid=(batch, seq // BLOCK),
        in_specs=[
            pl.BlockSpec(memory_space=pltpu.SMEM),
            pl.BlockSpec((None, BLOCK, qw), lambda b, i: (b, i, 0)),
            kv_spec(kblk, True), kv_spec(kblk, False), kv_spec(vblk, True), kv_spec(vblk, False),
        ],
        out_specs=pl.BlockSpec((None, BLOCK, qw), lambda b, i: (b, i, 0)),
        out_shape=jax.ShapeDtypeStruct((batch, seq, qw), BF16),
        compiler_params=_params("parallel", "arbitrary"),
        name="attn_c",
    )(sinks.astype(F32), view, view, view, view, view)
    return out.reshape(batch * seq, qw)


MM_TM = 1024
MM_TN = 512


def _ffn(x, norm, w_in, w_out, layer, in_place=True):
    h = _rmsnorm(x, norm[layer], BF16)
    hidden = _matmul_swiglu(h, w_in, layer, MM_TM, MM_TN)
    return _matmul_residual(hidden, w_out, layer, x, 0.5, MM_TM, MM_TN, in_place)


def _mixer_ab(x, norm, w_in, w_out, layer, batch, seq):
    h = _rmsnorm(x, norm, BF16)
    qkv = _matmul(h, w_in, layer, MM_TM, MM_TN)
    groups = [_attn_a_group(qkv, batch, seq, gi) for gi in range(len(A_PAIRS))]
    out_b = _stick_breaking(qkv, batch, seq)
    cat = _combine(groups, out_b)
    return _matmul_residual(cat, w_out, layer, x, 1.0, MM_TM, MM_TN)


def _mixer_c(x, norm, w_in, w_out, sinks, layer, batch, seq):
    h = _rmsnorm(x, norm, BF16)
    qkv = _matmul(h, w_in, layer, MM_TM, MM_TN)
    o = _attn_c(qkv, sinks, batch, seq)
    return _matmul_residual(o, w_out, layer, x, 1.0, MM_TM, MM_TN)


def kernel(x, ffn1_norm, ffn1_w_in, ffn1_w_out, mix_norm, ab_w_in, ab_w_out, c_w_in, c_w_out, c_sinks,
           ffn2_norm, ffn2_w_in, ffn2_w_out, final_norm):
    batch, seq, d = x.shape
    depth = ffn1_norm.shape[0]
    x = x.reshape(batch * seq, d)
    for l in range(depth):
        x = _ffn(x, ffn1_norm, ffn1_w_in, ffn1_w_out, l, in_place=l > 0)
        if l % 2 == 0:
            x = _mixer_ab(x, mix_norm[l], ab_w_in, ab_w_out, l // 2, batch, seq)
        else:
            x = _mixer_c(x, mix_norm[l], c_w_in, c_w_out, c_sinks[l // 2], l // 2, batch, seq)
        x = _ffn(x, ffn2_norm, ffn2_w_in, ffn2_w_out, l)
    return _rmsnorm(x, final_norm, F32).reshape(batch, seq, d)
```

```python
import functools

import numpy as np
import jax
import jax.numpy as jnp
from jax import lax
from jax.experimental import pallas as pl
from jax.experimental.pallas import tpu as pltpu

F32 = jnp.float32
BF16 = jnp.bfloat16

BLOCK = 128
RMS_EPS = 1e-6
HEAD_DIM_AB = 128
A_PAIRS = ((128, 1), (512, 4), (2048, 16))
GROUP_HEADS = 8
A_HEADS = GROUP_HEADS * len(A_PAIRS)
GROUP_W = GROUP_HEADS * HEAD_DIM_AB
C_HEAD_DIM = 64
C_Q_HEADS = 64
C_KV_HEADS = 8
C_WINDOW = 128
EXP_UNDERFLOW = -106.0

LANES = 128
LSE_LANES = LANES // GROUP_HEADS
VMEM_LIMIT_BYTES = 56 * 1024 * 1024


def _alibi_slopes(n):
    return [float(v) for v in np.asarray(2.0 ** (-8.0 * np.arange(1, n + 1) / n), dtype=np.float32)]


def _params(*sem):
    return pltpu.CompilerParams(dimension_semantics=sem, vmem_limit_bytes=VMEM_LIMIT_BYTES)


def _rmsnorm_kernel(x_ref, g_ref, o_ref):
    x = x_ref[...]
    ms = jnp.mean(x * x, axis=-1, keepdims=True)
    o_ref[...] = (x * lax.rsqrt(ms + RMS_EPS) * g_ref[...]).astype(o_ref.dtype)


def _rmsnorm(x, g, out_dtype, tm=256):
    t, d = x.shape
    return pl.pallas_call(
        _rmsnorm_kernel,
        grid=(t // tm,),
        in_specs=[pl.BlockSpec((tm, d), lambda i: (i, 0)), pl.BlockSpec((1, d), lambda i: (0, 0))],
        out_specs=pl.BlockSpec((tm, d), lambda i: (i, 0)),
        out_shape=jax.ShapeDtypeStruct((t, d), out_dtype),
        compiler_params=_params("parallel"),
        name="rmsnorm",
    )(x, g.reshape(1, d).astype(F32))


def _stream_weights(w_hbm, wb_ref, stage_ref, sem_ref, *, layer, col_offsets, tn):
    j, i = pl.program_id(0), pl.program_id(1)
    nj, ni = pl.num_programs(0), pl.num_programs(1)
    rows = wb_ref.shape[2] // ni
    cur = lax.rem(j, 2)

    def chunk_copy(block, chunk, w):
        src = w_hbm.at[layer, pl.ds(pl.multiple_of(chunk * rows, rows), rows),
                       pl.ds(pl.multiple_of((block + col_offsets[w]) * tn, tn), tn)]
        slot = lax.rem(chunk, 2)
        return pltpu.make_async_copy(src, stage_ref.at[slot, w], sem_ref.at[slot, w])

    def start(block, chunk):
        for w in range(len(col_offsets)):
            chunk_copy(block, chunk, w).start()

    def land(block, half, chunk):
        for w in range(len(col_offsets)):
            chunk_copy(block, chunk, w).wait()
            dst = wb_ref.at[half, w]
            dst[pl.ds(pl.multiple_of(chunk * rows, rows), rows), :] = stage_ref[lax.rem(chunk, 2), w].astype(BF16)

    @pl.when((j == 0) & (i == 0))
    def _():
        start(0, 0)

        def fetch(c, carry):
            @pl.when(c + 1 < ni)
            def _():
                start(0, c + 1)
            land(0, 0, c)
            return carry

        lax.fori_loop(0, ni, fetch, 0)

    @pl.when((j > 0) & (i == 0))
    def _():
        land(j, cur, ni - 1)

    @pl.when(j + 1 < nj)
    def _():
        start(j + 1, i)

        @pl.when(i > 0)
        def _():
            land(j + 1, 1 - cur, i - 1)

    return [wb_ref.at[cur, w] for w in range(len(col_offsets))]


def _mm_kernel(a_ref, w_hbm, o_ref, wb_ref, stage_ref, sem_ref, **stream):
    (w_ref,) = _stream_weights(w_hbm, wb_ref, stage_ref, sem_ref, **stream)
    o_ref[...] = jnp.dot(a_ref[...], w_ref[...], preferred_element_type=F32).astype(o_ref.dtype)


def _mm_swiglu_kernel(a_ref, w_hbm, o_ref, wb_ref, stage_ref, sem_ref, **stream):
    wg_ref, wu_ref = _stream_weights(w_hbm, wb_ref, stage_ref, sem_ref, **stream)
    a = a_ref[...]
    g = jnp.dot(a, wg_ref[...], preferred_element_type=F32)
    u = jnp.dot(a, wu_ref[...], preferred_element_type=F32)
    o_ref[...] = (g * jax.nn.sigmoid(g) * u).astype(o_ref.dtype)


def _mm_residual_kernel(a_ref, w_hbm, x_ref, o_ref, wb_ref, stage_ref, sem_ref, *, scale, **stream):
    (w_ref,) = _stream_weights(w_hbm, wb_ref, stage_ref, sem_ref, **stream)
    acc = jnp.dot(a_ref[...], w_ref[...], preferred_element_type=F32)
    o_ref[...] = x_ref[...] + scale * acc


def _weight_scratch(n_w, k, tn, ni):
    assert ni % 2 == 0 and k % ni == 0, "the staging ring alternates slots over an even number of row chunks"
    return [pltpu.VMEM((2, n_w, k, tn), BF16), pltpu.VMEM((2, n_w, k // ni, tn), F32),
            pltpu.SemaphoreType.DMA((2, n_w))]


def _matmul(a, w, layer, tm, tn):
    t, k = a.shape
    n = w.shape[2]
    return pl.pallas_call(
        functools.partial(_mm_kernel, layer=layer, col_offsets=(0,), tn=tn),
        grid=(n // tn, t // tm),
        in_specs=[pl.BlockSpec((tm, k), lambda j, i: (i, 0)), pl.BlockSpec(memory_space=pl.ANY)],
        out_specs=pl.BlockSpec((tm, tn), lambda j, i: (i, j)),
        out_shape=jax.ShapeDtypeStruct((t, n), BF16),
        scratch_shapes=_weight_scratch(1, k, tn, t // tm),
        compiler_params=_params("arbitrary", "arbitrary"),
        name="matmul",
    )(a, w)


def _matmul_swiglu(a, w_in, layer, tm, tn):
    t, k = a.shape
    f = w_in.shape[2] // 2
    nj = f // tn
    return pl.pallas_call(
        functools.partial(_mm_swiglu_kernel, layer=layer, col_offsets=(0, nj), tn=tn),
        grid=(nj, t // tm),
        in_specs=[pl.BlockSpec((tm, k), lambda j, i: (i, 0)), pl.BlockSpec(memory_space=pl.ANY)],
        out_specs=pl.BlockSpec((tm, tn), lambda j, i: (i, j)),
        out_shape=jax.ShapeDtypeStruct((t, f), BF16),
        scratch_shapes=_weight_scratch(2, k, tn, t // tm),
        compiler_params=_params("arbitrary", "arbitrary"),
        name="matmul_swiglu",
    )(a, w_in)


def _matmul_residual(a, w, layer, x, scale, tm, tn, in_place=True):
    t, k = a.shape
    n = w.shape[2]
    return pl.pallas_call(
        functools.partial(_mm_residual_kernel, scale=scale, layer=layer, col_offsets=(0,), tn=tn),
        grid=(n // tn, t // tm),
        in_specs=[
            pl.BlockSpec((tm, k), lambda j, i: (i, 0)),
            pl.BlockSpec(memory_space=pl.ANY),
            pl.BlockSpec((tm, tn), lambda j, i: (i, j)),
        ],
        out_specs=pl.BlockSpec((tm, tn), lambda j, i: (i, j)),
        out_shape=jax.ShapeDtypeStruct((t, n), F32),
        scratch_shapes=_weight_scratch(1, k, tn, t // tm),
        input_output_aliases={2: 0} if in_place else {},
        compiler_params=_params("arbitrary", "arbitrary"),
        name="matmul_residual",
    )(a, w, x)


def _band_masks(first_block_offset, max_dist, width):
    qi = lax.broadcasted_iota(jnp.int32, (BLOCK, width), 0)
    si = lax.broadcasted_iota(jnp.int32, (BLOCK, width), 1) & (2 * BLOCK - 1)
    dist = qi - si + BLOCK
    valid = (dist >= 0) & (dist <= max_dist) & (si + first_block_offset >= 0)
    return dist.astype(F32), valid


def _attn_a_kernel(q_ref, kp_ref, kc_ref, vp_ref, vc_ref, o_ref, l_ref, *, slopes, dist_scale, max_dist):
    i = pl.program_id(1)
    scale = HEAD_DIM_AB ** -0.5
    cols = [slice(h * HEAD_DIM_AB, (h + 1) * HEAD_DIM_AB) for h in range(GROUP_HEADS)]
    lane_head = lax.broadcasted_iota(jnp.int32, (BLOCK, LANES), 1) // LSE_LANES
    for sub in range(2):
        rows = slice(sub * BLOCK, (sub + 1) * BLOCK)
        dist, valid = _band_masks((2 * i + sub - 1) * BLOCK, max_dist, 2 * BLOCK)
        neg_dist = jnp.where(valid, -(dist_scale * dist), -jnp.inf)

        def keys(prev_ref, cur_ref, cs):
            prev = prev_ref[:, cs] if sub == 0 else cur_ref[:BLOCK, cs]
            return jnp.concatenate([prev, cur_ref[rows, cs]], axis=0)

        scores = [lax.dot_general(q_ref[rows, cs], keys(kp_ref, kc_ref, cs),
                                  (((1,), (1,)), ((), ())), preferred_element_type=F32) for cs in cols]
        probs, stats = [], []
        for h, s in enumerate(scores):
            s = s * scale + slopes[h] * neg_dist
            m = jnp.max(s, axis=-1, keepdims=True)
            p = jnp.exp(s - m)
            stats.append((m, jnp.sum(p, axis=-1, keepdims=True)))
            probs.append(p.astype(BF16))
        lse = jnp.zeros((BLOCK, LANES), F32)
        for h, (cs, p, (m, denom)) in enumerate(zip(cols, probs, stats)):
            o_ref[rows, cs] = (jnp.dot(p, keys(vp_ref, vc_ref, cs), preferred_element_type=F32) / denom).astype(o_ref.dtype)
            lse = jnp.where(lane_head == h, m + jnp.log(denom), lse)
        l_ref[rows, :] = lse


def _attn_a_group(qkv, batch, seq, gi):
    window, r = A_PAIRS[gi]
    a_blocks = A_HEADS * HEAD_DIM_AB // GROUP_W
    lr = seq // r
    slopes = _alibi_slopes(A_HEADS)[gi * GROUP_HEADS:(gi + 1) * GROUP_HEADS]
    if r == 1:
        src = qkv.reshape(batch, seq, qkv.shape[1])
        col_blocks = [section * a_blocks + gi for section in range(3)]
    else:
        cols = [qkv[:, (section * a_blocks + gi) * GROUP_W:(section * a_blocks + gi + 1) * GROUP_W] for section in range(3)]
        src = jnp.concatenate(cols, axis=1).reshape(batch, lr, r, 3 * GROUP_W)
        src = src.swapaxes(1, 2).reshape(batch * r, lr, 3 * GROUP_W)
        col_blocks = [0, 1, 2]

    def spec(section, prev):
        def idx(n, i):
            return (n, jnp.maximum(2 * i - 1, 0) if prev else i, col_blocks[section])
        return pl.BlockSpec((None, BLOCK if prev else 2 * BLOCK, GROUP_W), idx)

    o, l = pl.pallas_call(
        functools.partial(_attn_a_kernel, slopes=slopes, dist_scale=float(r), max_dist=window // r),
        grid=(batch * r, lr // (2 * BLOCK)),
        in_specs=[spec(0, False), spec(1, True), spec(1, False), spec(2, True), spec(2, False)],
        out_specs=[pl.BlockSpec((None, 2 * BLOCK, GROUP_W), lambda n, i: (n, i, 0)),
                   pl.BlockSpec((None, 2 * BLOCK, LANES), lambda n, i: (n, i, 0))],
        out_shape=[jax.ShapeDtypeStruct((batch * r, lr, GROUP_W), BF16),
                   jax.ShapeDtypeStruct((batch * r, lr, LANES), F32)],
        compiler_params=_params("parallel", "arbitrary"),
        name=f"attn_a_r{r}",
    )(src, src, src, src, src)

    def undilate(t):
        return t.reshape(batch, r, lr, t.shape[-1]).swapaxes(1, 2).reshape(batch * seq, t.shape[-1])

    return undilate(o), undilate(l)


def _sb_kernel(q_ref, k_ref, v_ref, o_ref, carry_ref, acc_ref, *, tile):
    iq = pl.program_id(1)
    scale = HEAD_DIM_AB ** -0.5
    row = lax.broadcasted_iota(jnp.int32, (tile, tile), 0)
    col = lax.broadcasted_iota(jnp.int32, (tile, tile), 1)
    suffix = jnp.concatenate([(row > col).astype(BF16), jnp.ones((tile, tile), BF16)], axis=1)
    suffix = jnp.concatenate([suffix, suffix], axis=0)
    causal = col < row
    heads = range(GROUP_HEADS)
    cols = [slice(h * HEAD_DIM_AB, (h + 1) * HEAD_DIM_AB) for h in heads]

    def step(j, diagonal):
        start = pl.multiple_of(j * tile, tile)
        zs = [lax.dot_general(q_ref[:, cs], k_ref[pl.ds(start, tile), cs], (((1,), (1,)), ((), ())),
                              preferred_element_type=F32) * scale for cs in cols]
        log_hits, splits = [], []
        for z in zs:
            tail = jnp.log(1.0 + jnp.exp(-jnp.abs(z)))
            log_keep = -(jnp.maximum(z, 0.0) + tail)
            log_hits.append(jnp.minimum(z, 0.0) - tail)
            if diagonal:
                log_keep = jnp.where(causal, log_keep, 0.0)
            hi = log_keep.astype(BF16)
            lo = (log_keep - hi.astype(F32)).astype(BF16)
            splits.append(jnp.concatenate([hi, lo], axis=1))
        sums = jnp.dot(jnp.concatenate(splits, axis=0), suffix, preferred_element_type=F32)
        for h, cs in zip(heads, cols):
            after = sums[h * tile:(h + 1) * tile, :tile]
            rowsum = sums[h * tile:(h + 1) * tile, tile:]
            if not diagonal:
                after = after + carry_ref[:, cs]
            a = jnp.exp(log_hits[h] + after)
            if diagonal:
                a = jnp.where(causal, a, 0.0)
            pv = jnp.dot(a.astype(BF16), v_ref[pl.ds(start, tile), cs], preferred_element_type=F32)
            if diagonal:
                acc_ref[:, cs] = pv
                carry_ref[:, cs] = rowsum
            else:
                acc_ref[:, cs] += pv
                carry_ref[:, cs] += rowsum

    def any_row_alive():
        c = carry_ref[:, cols[0]]
        for cs in cols[1:]:
            c = jnp.maximum(c, carry_ref[:, cs])
        return jnp.max(c) >= EXP_UNDERFLOW

    step(iq, True)

    def cond(state):
        return jnp.logical_and(state[0] <= iq, state[1])

    def body(state):
        step(iq - state[0], False)
        return state[0] + 1, any_row_alive()

    lax.while_loop(cond, body, (jnp.int32(1), any_row_alive()))
    o_ref[...] = acc_ref[...].astype(o_ref.dtype)


def _stick_breaking(qkv, batch, seq, tile=BLOCK):
    ncols = qkv.shape[1]
    view = qkv.reshape(batch, seq, ncols)
    base = 3 * A_HEADS * HEAD_DIM_AB // GROUP_W
    resident = dict(pipeline_mode=pl.Buffered(1))
    out = pl.pallas_call(
        functools.partial(_sb_kernel, tile=tile),
        grid=(batch, seq // tile),
        in_specs=[
            pl.BlockSpec((None, tile, GROUP_W), lambda b, i: (b, i, base)),
            pl.BlockSpec((None, seq, GROUP_W), lambda b, i: (b, 0, base + 1), **resident),
            pl.BlockSpec((None, seq, GROUP_W), lambda b, i: (b, 0, base + 2), **resident),
        ],
        out_specs=pl.BlockSpec((None, tile, GROUP_W), lambda b, i: (b, i, 0)),
        out_shape=jax.ShapeDtypeStruct((batch, seq, GROUP_W), BF16),
        scratch_shapes=[pltpu.VMEM((tile, GROUP_W), F32), pltpu.VMEM((tile, GROUP_W), F32)],
        compiler_params=_params("parallel", "arbitrary"),
        name="stick_breaking",
    )(view, view, view)
    return out.reshape(batch * seq, GROUP_W)


def _combine_kernel(o1_ref, l1_ref, o2_ref, l2_ref, o3_ref, l3_ref, ob_ref, out_ref):
    l1, l2, l3 = l1_ref[...], l2_ref[...], l3_ref[...]
    m = jnp.maximum(l1, jnp.maximum(l2, l3))
    e1, e2, e3 = jnp.exp(l1 - m), jnp.exp(l2 - m), jnp.exp(l3 - m)
    den = e1 + e2 + e3
    lane = lax.broadcasted_iota(jnp.int32, (2 * LANES, GROUP_W), 0) % LANES
    head = lax.broadcasted_iota(jnp.int32, (2 * LANES, GROUP_W), 1) // HEAD_DIM_AB
    spread = (lane == head * LSE_LANES).astype(BF16)
    num = jnp.zeros(o1_ref.shape, F32)
    for e, o_ref in ((e1, o1_ref), (e2, o2_ref), (e3, o3_ref)):
        w = e / den
        hi = w.astype(BF16)
        lo = (w - hi.astype(F32)).astype(BF16)
        wide = jnp.dot(jnp.concatenate([hi, lo], axis=1), spread, preferred_element_type=F32)
        num = num + wide * o_ref[...].astype(F32)
    out_ref[:, :GROUP_W] = num.astype(out_ref.dtype)
    out_ref[:, GROUP_W:] = ob_ref[...]


def _combine(groups, out_b, tm=512):
    t = out_b.shape[0]
    spec = pl.BlockSpec((tm, GROUP_W), lambda i: (i, 0))
    lse_spec = pl.BlockSpec((tm, LANES), lambda i: (i, 0))
    args = [a for pair in groups for a in pair] + [out_b]
    return pl.pallas_call(
        _combine_kernel,
        grid=(t // tm,),
        in_specs=[spec, lse_spec] * len(groups) + [spec],
        out_specs=pl.BlockSpec((tm, 2 * GROUP_W), lambda i: (i, 0)),
        out_shape=jax.ShapeDtypeStruct((t, 2 * GROUP_W), BF16),
        compiler_params=_params("parallel"),
        name="combine_ab",
    )(*args)


def _attn_c_kernel(sink_ref, q_ref, kp_ref, kc_ref, vp_ref, vc_ref, o_ref, *, slopes):
    i = pl.program_id(1)
    keys = 2 * BLOCK
    dist, valid = _band_masks((i - 1) * BLOCK, C_WINDOW - 1, keys)
    neg_dist = jnp.where(valid, -(1.0 * dist), -jnp.inf)
    low_in = lax.broadcasted_iota(jnp.int32, (keys, LANES), 1) < C_HEAD_DIM
    low_out = lax.broadcasted_iota(jnp.int32, (BLOCK, LANES), 1) < C_HEAD_DIM
    scale = C_HEAD_DIM ** -0.5
    group = C_Q_HEADS // C_KV_HEADS

    def block_diag(x, x_rolled, upper):
        src_low, src_high = (x_rolled, x) if upper else (x, x_rolled)
        zero = jnp.zeros_like(x)
        return jnp.concatenate([jnp.where(low_in, src_low, zero), jnp.where(low_in, zero, src_high)], axis=0)

    for c in range(C_KV_HEADS * C_HEAD_DIM // LANES):
        cs = slice(c * LANES, (c + 1) * LANES)
        kch = jnp.concatenate([kp_ref[:, cs], kc_ref[:, cs]], axis=0)
        vch = jnp.concatenate([vp_ref[:, cs], vc_ref[:, cs]], axis=0)
        kro = pltpu.roll(kch.astype(F32), C_HEAD_DIM, 1).astype(BF16)
        vro = pltpu.roll(vch.astype(F32), C_HEAD_DIM, 1).astype(BF16)
        for e in range(LANES // C_HEAD_DIM):
            kv = c * (LANES // C_HEAD_DIM) + e
            kb = block_diag(kch, kro, e == 1)
            vb = block_diag(vch, vro, e == 1)
            pairs = [kv * group + 2 * p for p in range(group // 2)]
            qcols = [slice(h0 * C_HEAD_DIM, (h0 + 2) * C_HEAD_DIM) for h0 in pairs]
            scores = [lax.dot_general(q_ref[:, qs], kb, (((1,), (1,)), ((), ())), preferred_element_type=F32)
                      for qs in qcols]
            probs, denoms = [], []
            for h0, s in zip(pairs, scores):
                halves, dens = [], []
                for t in range(2):
                    sh = s[:, t * keys:(t + 1) * keys] * scale + slopes[h0 + t] * neg_dist
                    sink = sink_ref[h0 + t]
                    m = jnp.maximum(jnp.max(sh, axis=-1, keepdims=True), sink)
                    pr = jnp.exp(sh - m)
                    dens.append(jnp.sum(pr, axis=-1, keepdims=True) + jnp.exp(sink - m))
                    halves.append(pr.astype(BF16))
                probs.append(jnp.concatenate(halves, axis=1))
                denoms.append(dens)
            for qs, pr, (d0, d1) in zip(qcols, probs, denoms):
                o = jnp.dot(pr, vb, preferred_element_type=F32)
                o_ref[:, qs] = (o / jnp.where(low_out, d0, d1)).astype(o_ref.dtype)


def _attn_c(qkv, sinks, batch, seq):
    ncols = qkv.shape[1]
    qw = C_Q_HEADS * C_HEAD_DIM
    kvw = C_KV_HEADS * C_HEAD_DIM
    view = qkv.reshape(batch, seq, ncols)
    kblk, vblk = qw // kvw, qw // kvw + 1

    def kv_spec(blk, prev):
        def idx(b, i):
            return (b, jnp.maximum(i - 1, 0) if prev else i, blk)
        return pl.BlockSpec((None, BLOCK, kvw), idx)

    out = pl.pallas_call(
        functools.partial(_attn_c_kernel, slopes=_alibi_slopes(C_Q_HEADS)),
        grid=(batch, seq // BLOCK),
        in_specs=[
            pl.BlockSpec(memory_space=pltpu.SMEM),
            pl.BlockSpec((None, BLOCK, qw), lambda b, i: (b, i, 0)),
            kv_spec(kblk, True), kv_spec(kblk, False), kv_spec(vblk, True), kv_spec(vblk, False),
        ],
        out_specs=pl.BlockSpec((None, BLOCK, qw), lambda b, i: (b, i, 0)),
        out_shape=jax.ShapeDtypeStruct((batch, seq, qw), BF16),
        compiler_params=_params("parallel", "arbitrary"),
        name="attn_c",
    )(sinks.astype(F32), view, view, view, view, view)
    return out.reshape(batch * seq, qw)


MM_TM = 1024
MM_TN = 512


def _ffn(x, norm, w_in, w_out, layer, in_place=True):
    h = _rmsnorm(x, norm[layer], BF16)
    hidden = _matmul_swiglu(h, w_in, layer, MM_TM, MM_TN)
    return _matmul_residual(hidden, w_out, layer, x, 0.5, MM_TM, MM_TN, in_place)


def _mixer_ab(x, norm, w_in, w_out, layer, batch, seq):
    h = _rmsnorm(x, norm, BF16)
    qkv = _matmul(h, w_in, layer, MM_TM, 2 * MM_TN)
    groups = [_attn_a_group(qkv, batch, seq, gi) for gi in range(len(A_PAIRS))]
    out_b = _stick_breaking(qkv, batch, seq)
    cat = _combine(groups, out_b)
    return _matmul_residual(cat, w_out, layer, x, 1.0, MM_TM, MM_TN)


def _mixer_c(x, norm, w_in, w_out, sinks, layer, batch, seq):
    h = _rmsnorm(x, norm, BF16)
    qkv = _matmul(h, w_in, layer, MM_TM, 2 * MM_TN)
    o = _attn_c(qkv, sinks, batch, seq)
    return _matmul_residual(o, w_out, layer, x, 1.0, MM_TM, MM_TN)


def kernel(x, ffn1_norm, ffn1_w_in, ffn1_w_out, mix_norm, ab_w_in, ab_w_out, c_w_in, c_w_out, c_sinks,
           ffn2_norm, ffn2_w_in, ffn2_w_out, final_norm):
    batch, seq, d = x.shape
    depth = ffn1_norm.shape[0]
    x = x.reshape(batch * seq, d)
    for l in range(depth):
        x = _ffn(x, ffn1_norm, ffn1_w_in, ffn1_w_out, l, in_place=l > 0)
        if l % 2 == 0:
            x = _mixer_ab(x, mix_norm[l], ab_w_in, ab_w_out, l // 2, batch, seq)
        else:
            x = _mixer_c(x, mix_norm[l], c_w_in, c_w_out, c_sinks[l // 2], l // 2, batch, seq)
        x = _ffn(x, ffn2_norm, ffn2_w_in, ffn2_w_out, l)
    return _rmsnorm(x, final_norm, F32).reshape(batch, seq, d)
```
